```python
import math
import jax, jax.numpy as jnp
from jax import lax
import numpy as np

D_MODEL = 1024
BATCH = 4
SEQ = 8192
DEPTH = 4

SB_HEADS = 8
SB_HEAD_DIM = 64
SB_WIDTH = SB_HEADS * SB_HEAD_DIM
Q_BLOCK = 128
DN_HEADS = 4
DN_HEAD_DIM = 128
DN_WIDTH = DN_HEADS * DN_HEAD_DIM
CONV_K = 4
CHUNK = 64
N_EXPERTS = 16
N_GROUPS = 4
EXPERTS_PER_GROUP = N_EXPERTS // N_GROUPS
TOP_K = 2
D_EXPERT = 512
ROUTE_BLOCK = 128
DEEPNORM_ALPHA = (2 * DEPTH) ** 0.25
DEEPNORM_BETA = (8 * DEPTH) ** -0.25
LN_EPS = 1e-5
NORM_EPS = 1e-6

PROJ_SPLITS = (SB_WIDTH, SB_WIDTH, SB_WIDTH, DN_WIDTH, DN_WIDTH, DN_WIDTH, DN_WIDTH, DN_HEADS, DN_HEADS, D_MODEL, D_MODEL)
VALUE_COLS = (2, 5)
PROJ_WIDTH = sum(PROJ_SPLITS)

kernel_name = "stickbreak_gdn_gated_hybrid_moe"


def _split(t, sizes):
    offs = np.cumsum(sizes)[:-1].tolist()
    return jnp.split(t, offs, axis=-1)


def layer_norm(x, g, b):
    xf = x.astype(jnp.float32)
    mu = jnp.mean(xf, axis=-1, keepdims=True)
    xc = xf - mu
    var = jnp.mean(xc * xc, axis=-1, keepdims=True)
    return (xc * lax.rsqrt(var + LN_EPS) * g.astype(jnp.float32) + b.astype(jnp.float32)).astype(x.dtype)


def l2_normalize(t):
    return t * lax.rsqrt(jnp.sum(t * t, axis=-1, keepdims=True) + NORM_EPS)


def causal_depthwise_conv(u, w):
    c = u.shape[-1]
    return lax.conv_general_dilated(u, w[:, None, :].astype(u.dtype), window_strides=(1,), padding=[(w.shape[0] - 1, 0)], dimension_numbers=("NWC", "WIO", "NWC"), feature_group_count=c)


def stick_breaking_attention(q, k, v):
    s_len, d = q.shape[2], q.shape[3]
    scale = d ** -0.5
    qf, kf = q.astype(jnp.float32), k.astype(jnp.float32)
    outs = []
    for i in range(s_len // Q_BLOCK):
        start, end = i * Q_BLOCK, (i + 1) * Q_BLOCK
        z = jnp.einsum("bhtd,bhsd->bhts", qf[:, :, start:end], kf[:, :, :end]) * scale
        t_idx = start + jnp.arange(Q_BLOCK)[:, None]
        s_idx = jnp.arange(end)[None, :]
        mask = s_idx < t_idx
        log_beta = jax.nn.log_sigmoid(z)
        log_keep = jnp.where(mask, jax.nn.log_sigmoid(-z), 0.0)
        rest = lax.cumsum(log_keep, axis=3, reverse=True) - log_keep
        a = jnp.where(mask, jnp.exp(log_beta + rest), 0.0)
        outs.append(jnp.einsum("bhts,bhsd->bhtd", a.astype(v.dtype), v[:, :, :end]))
    return jnp.concatenate(outs, axis=2)


def gated_delta_rule(q, k, v, beta, g):
    bn, s_len, h, dk = q.shape
    dv = v.shape[-1]
    n = s_len // CHUNK
    chunks = lambda t: t.reshape(bn, n, CHUNK, h, -1).transpose(0, 3, 1, 2, 4)
    q, k, v = chunks(q), chunks(k), chunks(v)
    beta = beta.reshape(bn, n, CHUNK, h).transpose(0, 3, 1, 2)
    gc = jnp.cumsum(g.reshape(bn, n, CHUNK, h).transpose(0, 3, 1, 2), axis=-1)
    idx = jnp.arange(CHUNK)
    incl = idx[:, None] >= idx[None, :]
    strict = idx[:, None] > idx[None, :]
    gamma = jnp.exp(jnp.where(incl, gc[..., :, None] - gc[..., None, :], -jnp.inf))
    kk = jnp.einsum("bhnid,bhnjd->bhnij", k, k)
    a_low = jnp.where(strict, beta[..., :, None] * kk * gamma, 0.0)
    t_mat = a_low + jnp.eye(CHUNK, dtype=a_low.dtype)
    solve = lambda rhs: lax.linalg.triangular_solve(t_mat, rhs, left_side=True, lower=True, unit_diagonal=True)
    u = solve(beta[..., None] * v)
    w = solve(beta[..., None] * k * jnp.exp(gc)[..., None])
    qk = jnp.einsum("bhnid,bhnjd->bhnij", q, k) * gamma
    q_dec = q * jnp.exp(gc)[..., None]
    k_dec = k * jnp.exp(gc[..., -1:] - gc)[..., None]
    last = jnp.exp(gc[..., -1])

    def step(state, inp):
        u_c, w_c, qk_c, qd_c, kd_c, last_c = inp
        v_new = u_c - jnp.einsum("bhcd,bhde->bhce", w_c, state)
        o_c = jnp.einsum("bhcd,bhde->bhce", qd_c, state) + jnp.einsum("bhij,bhje->bhie", qk_c, v_new)
        state = last_c[..., None, None] * state + jnp.einsum("bhcd,bhce->bhde", kd_c, v_new)
        return state, o_c

    mv = lambda t: jnp.moveaxis(t, 2, 0)
    s0 = jnp.zeros((bn, h, dk, dv), jnp.float32)
    _, o = lax.scan(step, s0, (mv(u), mv(w), mv(qk), mv(q_dec), mv(k_dec), mv(last)))
    return o.transpose(1, 0, 3, 2, 4).reshape(bn, s_len, h, dv)


def hybrid_mixer(x, w_in, conv_w, a_log, dt_bias, onorm_g, w_br_a, w_br_b, w_o):
    bn, s_len, _ = x.shape
    proj = x @ w_in
    q_a, k_a, v_a, q_b, k_b, v_b, z_b, b_b, a_b, g_a, g_b = _split(proj, PROJ_SPLITS)
    sb = lambda t: t.reshape(bn, s_len, SB_HEADS, SB_HEAD_DIM).transpose(0, 2, 1, 3)
    o_a = stick_breaking_attention(sb(q_a), sb(k_a), sb(v_a))
    o_a = o_a.transpose(0, 2, 1, 3).reshape(bn, s_len, SB_WIDTH)
    qkv = jax.nn.silu(causal_depthwise_conv(jnp.concatenate([q_b, k_b, v_b], axis=-1), conv_w)).astype(jnp.float32)
    q_b, k_b, v_b = _split(qkv, (DN_WIDTH, DN_WIDTH, DN_WIDTH))
    dn = lambda t: t.reshape(bn, s_len, DN_HEADS, DN_HEAD_DIM)
    q_b = l2_normalize(dn(q_b)) * (DN_HEAD_DIM ** -0.5)
    k_b = l2_normalize(dn(k_b))
    beta = jax.nn.sigmoid(b_b.astype(jnp.float32))
    g = -jnp.exp(a_log.astype(jnp.float32)) * jax.nn.softplus(a_b.astype(jnp.float32) + dt_bias.astype(jnp.float32))
    o_b = gated_delta_rule(q_b, k_b, dn(v_b), beta, g)
    o_b = o_b * lax.rsqrt(jnp.mean(o_b * o_b, axis=-1, keepdims=True) + NORM_EPS) * onorm_g.astype(jnp.float32)
    o_b = (o_b * jax.nn.silu(dn(z_b).astype(jnp.float32))).reshape(bn, s_len, DN_WIDTH).astype(x.dtype)
    merged = jax.nn.sigmoid(g_a) * (o_a @ w_br_a) + jax.nn.sigmoid(g_b) * (o_b @ w_br_b)
    return merged @ w_o


def grouped_moe(h, w_router, router_bias, w_gate, w_up, w_down):
    bn, s_len, d = h.shape
    n_tok = bn * s_len
    xf = h.reshape(n_tok, d)
    probs = jax.nn.softmax(xf.astype(jnp.float32) @ w_router.astype(jnp.float32), axis=-1)
    sel = (probs + router_bias.astype(jnp.float32)).reshape(n_tok, N_GROUPS, EXPERTS_PER_GROUP)
    grp_score = jnp.sum(lax.top_k(sel, 2)[0], axis=-1)
    g_idx = jnp.argmax(grp_score, axis=-1)
    in_grp = jnp.take_along_axis(sel, g_idx[:, None, None], axis=1)[:, 0]
    _, local = lax.top_k(in_grp, TOP_K)
    e_idx = g_idx[:, None] * EXPERTS_PER_GROUP + local
    wts = jnp.take_along_axis(probs, e_idx, axis=-1)
    wts = wts / jnp.sum(wts, axis=-1, keepdims=True)
    m = n_tok * TOP_K
    e_flat = e_idx.reshape(m).astype(jnp.int32)
    tok_flat = jnp.repeat(jnp.arange(n_tok, dtype=jnp.int32), TOP_K)
    w_flat = wts.reshape(m)
    order = jnp.argsort(e_flat)
    e_sorted = e_flat[order]
    counts = jnp.zeros((N_EXPERTS,), jnp.int32).at[e_flat].add(1)
    padded = (counts + ROUTE_BLOCK - 1) // ROUTE_BLOCK * ROUTE_BLOCK
    starts = jnp.cumsum(counts) - counts
    pends = jnp.cumsum(padded)
    pstarts = pends - padded
    dest = pstarts[e_sorted] + (jnp.arange(m, dtype=jnp.int32) - starts[e_sorted])
    p_rows = m + N_EXPERTS * ROUTE_BLOCK
    row_tok = jnp.full((p_rows,), n_tok, jnp.int32).at[dest].set(tok_flat[order])
    row_w = jnp.zeros((p_rows,), jnp.float32).at[dest].set(w_flat[order])
    n_blk = p_rows // ROUTE_BLOCK
    blk_exp = jnp.minimum(jnp.searchsorted(pends, jnp.arange(n_blk, dtype=jnp.int32) * ROUTE_BLOCK, side="right"), N_EXPERTS - 1)
    xs = jnp.concatenate([xf, jnp.zeros((1, d), xf.dtype)], axis=0)[row_tok].reshape(n_blk, ROUTE_BLOCK, d)

    def expert_block(args):
        xb, e = args
        hid = jax.nn.silu(xb @ w_gate[e]) * (xb @ w_up[e])
        return hid @ w_down[e]

    ys = lax.map(expert_block, (xs, blk_exp)).reshape(p_rows, d)
    ys = ys * row_w[:, None].astype(ys.dtype)
    out = jax.ops.segment_sum(ys, row_tok, num_segments=n_tok + 1)[:n_tok]
    return out.reshape(bn, s_len, d).astype(h.dtype)


def setup_inputs(seed: int = 0) -> dict:
    key = jax.random.key(seed)
    ks = jax.random.split(key, 20)
    f32 = jnp.float32
    L, D, E, F = DEPTH, D_MODEL, N_EXPERTS, D_EXPERT
    nrm = lambda k, shape: jax.random.normal(k, shape, f32)
    x = nrm(ks[0], (BATCH, SEQ, D))
    col_scale = jnp.concatenate([jnp.full((wd,), DEEPNORM_BETA if i in VALUE_COLS else 1.0, f32) for i, wd in enumerate(PROJ_SPLITS)])
    w_in = nrm(ks[1], (L, D, PROJ_WIDTH)) * (D ** -0.5) * col_scale
    conv_w = nrm(ks[2], (L, CONV_K, 3 * DN_WIDTH)) * (CONV_K ** -0.5)
    a_log = jnp.log(jax.random.uniform(ks[3], (L, DN_HEADS), f32, 1.0, 16.0))
    dt = jnp.exp(jax.random.uniform(ks[4], (L, DN_HEADS), f32, math.log(1e-3), math.log(1e-1)))
    dt_bias = dt + jnp.log(-jnp.expm1(-dt))
    onorm_g = 1.0 + 0.02 * nrm(ks[5], (L, DN_HEAD_DIM))
    w_br_a = nrm(ks[6], (L, SB_WIDTH, D)) * (SB_WIDTH ** -0.5) * DEEPNORM_BETA
    w_br_b = nrm(ks[7], (L, DN_WIDTH, D)) * (DN_WIDTH ** -0.5) * DEEPNORM_BETA
    w_o = nrm(ks[8], (L, D, D)) * (D ** -0.5) * DEEPNORM_BETA
    ln1_g = 1.0 + 0.02 * nrm(ks[9], (L, D))
    ln1_b = 0.01 * nrm(ks[10], (L, D))
    w_router = nrm(ks[11], (D, E)) * (D ** -0.5)
    router_bias = 0.01 * nrm(ks[12], (E,))
    w_gate = nrm(ks[13], (L, E, D, F)) * (D ** -0.5) * DEEPNORM_BETA
    w_up = nrm(ks[14], (L, E, D, F)) * (D ** -0.5) * DEEPNORM_BETA
    w_down = nrm(ks[15], (L, E, F, D)) * (F ** -0.5) * DEEPNORM_BETA
    ln2_g = 1.0 + 0.02 * nrm(ks[16], (L, D))
    ln2_b = 0.01 * nrm(ks[17], (L, D))
    return {"x": x, "w_in": w_in, "conv_w": conv_w, "a_log": a_log, "dt_bias": dt_bias, "onorm_g": onorm_g, "w_br_a": w_br_a, "w_br_b": w_br_b, "w_o": w_o, "ln1_g": ln1_g, "ln1_b": ln1_b, "w_router": w_router, "router_bias": router_bias, "w_gate": w_gate, "w_up": w_up, "w_down": w_down, "ln2_g": ln2_g, "ln2_b": ln2_b}


def reference(x, w_in, conv_w, a_log, dt_bias, onorm_g, w_br_a, w_br_b, w_o, ln1_g, ln1_b, w_router, router_bias, w_gate, w_up, w_down, ln2_g, ln2_b):
    for l in range(DEPTH):
        mix = hybrid_mixer(x, w_in[l], conv_w[l], a_log[l], dt_bias[l], onorm_g[l], w_br_a[l], w_br_b[l], w_o[l])
        x = layer_norm(DEEPNORM_ALPHA * x + mix, ln1_g[l], ln1_b[l])
        ffn = grouped_moe(x, w_router, router_bias, w_gate[l], w_up[l], w_down[l])
        x = layer_norm(DEEPNORM_ALPHA * x + ffn, ln2_g[l], ln2_b[l])
    return x
```

```python
import functools

import jax
import jax.numpy as jnp
from jax import lax
from jax.experimental import pallas as pl
from jax.experimental.pallas import tpu as pltpu

D_MODEL = 1024
SB_HEADS = 8
SB_HEAD_DIM = 64
SB_WIDTH = SB_HEADS * SB_HEAD_DIM
DN_HEADS = 4
DN_HEAD_DIM = 128
DN_WIDTH = DN_HEADS * DN_HEAD_DIM
CONV_K = 4
CHUNK = 64
N_EXPERTS = 16
N_GROUPS = 4
EXPERTS_PER_GROUP = N_EXPERTS // N_GROUPS
TOP_K = 2
D_EXPERT = 512
LN_EPS = 1e-5
NORM_EPS = 1e-6

LANES = 128
VMEM_LIMIT = 48 * 1024 * 1024

PROJ_TM = 512
PROJ_TN = 512
SB_TILE = 128
GDN_TS = 512
MERGE_TM = 256
ROUTE_BLOCK = 256
COMBINE_TM = 256
SB_SKIP_LOG = -100.0

BF16 = jnp.bfloat16
F32 = jnp.float32
NT_DIMS = (((1,), (1,)), ((), ()))
TN_DIMS = (((0,), (0,)), ((), ()))


def _params(*semantics):
    return pltpu.CompilerParams(dimension_semantics=semantics, vmem_limit_bytes=VMEM_LIMIT)


def _sigmoid(x):
    return 1.0 / (1.0 + jnp.exp(-x))


def _softplus(x):
    return jnp.maximum(x, 0.0) + jnp.log(1.0 + jnp.exp(-jnp.abs(x)))


def _layer_norm(y, g, b):
    mu = jnp.mean(y, axis=-1, keepdims=True)
    yc = y - mu
    var = jnp.mean(yc * yc, axis=-1, keepdims=True)
    return yc * lax.rsqrt(var + LN_EPS) * g + b


def _mm_kernel(x_ref, w_ref, o_ref):
    o_ref[...] = jnp.dot(x_ref[...].astype(BF16), w_ref[...],
                         preferred_element_type=F32).astype(o_ref.dtype)


def _matmul(x, w, out_dtype, tn):
    m, k = x.shape
    n = w.shape[1]
    return pl.pallas_call(
        _mm_kernel,
        grid=(m // PROJ_TM, n // tn),
        in_specs=[pl.BlockSpec((PROJ_TM, k), lambda i, j: (i, 0)),
                  pl.BlockSpec((k, tn), lambda i, j: (0, j))],
        out_specs=pl.BlockSpec((PROJ_TM, tn), lambda i, j: (i, j)),
        out_shape=jax.ShapeDtypeStruct((m, n), out_dtype),
        compiler_params=_params("parallel", "parallel"),
        name="proj",
    )(x, w)


def _sb_kernel(q_ref, k_ref, v_ref, o_ref, acc_ref, c_ref, *, tile):
    i = pl.program_id(2)
    lane = lax.broadcasted_iota(jnp.int32, (1, LANES), 1)
    head_a = lane < SB_HEAD_DIM
    q = q_ref[...] * jnp.asarray(SB_HEAD_DIM ** -0.5, q_ref.dtype)
    zero = jnp.zeros_like(q)
    q_heads = (jnp.where(head_a, q, zero), jnp.where(head_a, zero, q))
    acc_ref[...] = jnp.zeros_like(acc_ref)
    c_ref[...] = jnp.zeros_like(c_ref)
    row = lax.broadcasted_iota(jnp.int32, (tile, tile), 0)
    col = lax.broadcasted_iota(jnp.int32, (tile, tile), 1)
    srow = lax.broadcasted_iota(jnp.int32, (tile, tile + LANES), 0)
    scol = lax.broadcasted_iota(jnp.int32, (tile, tile + LANES), 1)
    later = jnp.where((scol >= tile) | (srow > scol), 1.0, 0.0).astype(BF16)

    def body(carry):
        j, _ = carry
        start = pl.multiple_of(j * tile, tile)
        k = k_ref[pl.ds(start, tile), :]
        v = v_ref[pl.ds(start, tile), :]
        causal = col + (j - i) * tile < row
        cmax = None
        for h in range(2):
            z = lax.dot_general(q_heads[h], k, NT_DIMS, preferred_element_type=F32)
            log_beta = jnp.minimum(z, 0.0) - jnp.log(1.0 + jnp.exp(-jnp.abs(z)))
            log_keep = jnp.where(causal, log_beta - z, 0.0)
            sums = jnp.dot(log_keep.astype(BF16), later, preferred_element_type=F32)
            a = jnp.where(causal, jnp.exp(log_beta + sums[:, :tile]), 0.0)
            pv = jnp.dot(a.astype(BF16), v, preferred_element_type=F32)
            c = c_ref[h]
            acc_ref[h] += jnp.exp(c) * pv
            c = c + sums[:, tile:]
            c_ref[h] = c
            m = jnp.max(c)
            cmax = m if cmax is None else jnp.maximum(cmax, m)
        return j - 1, cmax

    lax.while_loop(lambda cr: (cr[0] >= 0) & (cr[1] > SB_SKIP_LOG), body, (i, jnp.float32(0.0)))
    o_ref[...] = jnp.where(head_a, acc_ref[0], acc_ref[1]).astype(o_ref.dtype)


def _stick_breaking(qkv, batch, seq):
    pairs = SB_WIDTH // LANES
    return pl.pallas_call(
        functools.partial(_sb_kernel, tile=SB_TILE),
        grid=(batch, pairs, seq // SB_TILE),
        in_specs=[pl.BlockSpec((None, SB_TILE, LANES), lambda b, p, i: (b, i, p)),
                  pl.BlockSpec((None, seq, LANES), lambda b, p, i: (b, 0, pairs + p)),
                  pl.BlockSpec((None, seq, LANES), lambda b, p, i: (b, 0, 2 * pairs + p))],
        out_specs=pl.BlockSpec((None, SB_TILE, LANES), lambda b, p, i: (b, i, p)),
        out_shape=jax.ShapeDtypeStruct((batch, seq, SB_WIDTH), BF16),
        scratch_shapes=[pltpu.VMEM((2, SB_TILE, LANES), F32), pltpu.VMEM((2, SB_TILE, LANES), F32)],
        compiler_params=_params("parallel", "parallel", "parallel"),
        name="stick_breaking",
    )(qkv, qkv, qkv)


def _unit_lower_inverse(a, rows, cols):
    eye = jnp.where(rows == cols, 1.0, 0.0)
    x = eye - jnp.where(rows // 2 == cols // 2, a, 0.0)
    blk = 2
    while blk < CHUNK:
        low = jnp.where((rows // (2 * blk) == cols // (2 * blk)) & (rows // blk != cols // blk), a, 0.0)
        t = jnp.dot(low, x, preferred_element_type=F32, precision=lax.Precision.HIGHEST)
        x = x - jnp.dot(x, t, preferred_element_type=F32, precision=lax.Precision.HIGHEST)
        blk *= 2
    return x


def _gdn_kernel(qkv_ref, z_ref, s_ref, cw_ref, alog_ref, dt_ref, og_ref, o_ref,
                ubuf, cbuf, state_ref, *, ts):
    si = pl.program_id(1)
    tail = 8

    @pl.when(si == 0)
    def _():
        state_ref[...] = jnp.zeros_like(state_ref)
        ubuf[0:tail, :] = jnp.zeros((tail, 3 * DN_WIDTH), F32)

    @pl.when(si > 0)
    def _():
        ubuf[0:tail, :] = ubuf[ts:ts + tail, :]

    ubuf[tail:tail + ts, :] = qkv_ref[...]
    conv = cw_ref[0:1, :] * ubuf[tail - 3:tail - 3 + ts, :]
    for j in range(1, CONV_K):
        conv = conv + cw_ref[j:j + 1, :] * ubuf[tail - 3 + j:tail - 3 + j + ts, :]
    cbuf[...] = conv * _sigmoid(conv)

    rows = lax.broadcasted_iota(jnp.int32, (CHUNK, CHUNK), 0)
    cols = lax.broadcasted_iota(jnp.int32, (CHUNK, CHUNK), 1)
    incl = rows >= cols
    strict = rows > cols
    tri = jnp.where(incl, 1.0, 0.0)
    neg_a = -jnp.exp(alog_ref[...])
    dt_bias = dt_ref[...]
    onorm_g = og_ref[...]

    def chunk_body(c, _):
        r0 = pl.multiple_of(c * CHUNK, CHUNK)
        sm = s_ref[pl.ds(r0, CHUNK), :]
        beta_all = _sigmoid(sm)
        g_all = neg_a * _softplus(sm + dt_bias)
        gc_all = jnp.dot(tri, g_all, preferred_element_type=F32, precision=lax.Precision.HIGHEST)
        gc_t = gc_all.T
        for h in range(DN_HEADS):
            lo, hi = h * DN_HEAD_DIM, (h + 1) * DN_HEAD_DIM
            q = cbuf[pl.ds(r0, CHUNK), lo:hi]
            k = cbuf[pl.ds(r0, CHUNK), DN_WIDTH + lo:DN_WIDTH + hi]
            v = cbuf[pl.ds(r0, CHUNK), 2 * DN_WIDTH + lo:2 * DN_WIDTH + hi]
            q = q * lax.rsqrt(jnp.sum(q * q, axis=-1, keepdims=True) + NORM_EPS) * (DN_HEAD_DIM ** -0.5)
            k = k * lax.rsqrt(jnp.sum(k * k, axis=-1, keepdims=True) + NORM_EPS)
            beta = beta_all[:, h:h + 1]
            gcol = gc_all[:, DN_HEADS + h:DN_HEADS + h + 1]
            grow = gc_t[DN_HEADS + h:DN_HEADS + h + 1, :]
            glast = gc_all[CHUNK - 1:CHUNK, DN_HEADS + h:DN_HEADS + h + 1]
            gamma = jnp.exp(jnp.where(incl, gcol - grow, -jnp.inf))
            kb = k.astype(BF16)
            kk = lax.dot_general(kb, kb, NT_DIMS, preferred_element_type=F32)
            a_low = jnp.where(strict, beta * kk * gamma, 0.0)
            t_inv = _unit_lower_inverse(a_low, rows, cols)
            eg = jnp.exp(gcol)
            rhs = jnp.concatenate([beta * v, beta * k * eg], axis=1)
            uw = jnp.dot(t_inv, rhs, preferred_element_type=F32, precision=lax.Precision.HIGHEST)
            u, w = uw[:, :DN_HEAD_DIM], uw[:, DN_HEAD_DIM:]
            qk = lax.dot_general(q.astype(BF16), kb, NT_DIMS, preferred_element_type=F32) * gamma
            q_dec = (q * eg).astype(BF16)
            k_dec = (k * jnp.exp(glast - gcol)).astype(BF16)
            state = state_ref[h]
            sb = state.astype(BF16)
            v_new = u - jnp.dot(w.astype(BF16), sb, preferred_element_type=F32)
            vb = v_new.astype(BF16)
            o = (jnp.dot(q_dec, sb, preferred_element_type=F32)
                 + jnp.dot(qk.astype(BF16), vb, preferred_element_type=F32))
            state_ref[h] = jnp.exp(glast) * state + lax.dot_general(k_dec, vb, TN_DIMS,
                                                                    preferred_element_type=F32)
            o = o * lax.rsqrt(jnp.mean(o * o, axis=-1, keepdims=True) + NORM_EPS) * onorm_g
            zz = z_ref[pl.ds(r0, CHUNK), lo:hi]
            o_ref[pl.ds(r0, CHUNK), lo:hi] = (o * (zz * _sigmoid(zz))).astype(o_ref.dtype)
        return 0

    lax.fori_loop(0, ts // CHUNK, chunk_body, 0)


def _gated_deltanet(pb, ps, conv_w, alog_row, dt_row, onorm_row, batch, seq):
    ts = GDN_TS
    qkv_blocks = 3 * DN_WIDTH // DN_WIDTH
    return pl.pallas_call(
        functools.partial(_gdn_kernel, ts=ts),
        grid=(batch, seq // ts),
        in_specs=[pl.BlockSpec((None, ts, 3 * DN_WIDTH), lambda b, s: (b, s, 0)),
                  pl.BlockSpec((None, ts, DN_WIDTH), lambda b, s: (b, s, qkv_blocks)),
                  pl.BlockSpec((None, ts, LANES), lambda b, s: (b, s, 0)),
                  pl.BlockSpec((CONV_K, 3 * DN_WIDTH), lambda b, s: (0, 0)),
                  pl.BlockSpec((1, LANES), lambda b, s: (0, 0)),
                  pl.BlockSpec((1, LANES), lambda b, s: (0, 0)),
                  pl.BlockSpec((1, DN_HEAD_DIM), lambda b, s: (0, 0))],
        out_specs=pl.BlockSpec((None, ts, DN_WIDTH), lambda b, s: (b, s, 0)),
        out_shape=jax.ShapeDtypeStruct((batch, seq, DN_WIDTH), BF16),
        scratch_shapes=[pltpu.VMEM((ts + 8, 3 * DN_WIDTH), F32),
                        pltpu.VMEM((ts, 3 * DN_WIDTH), F32),
                        pltpu.VMEM((DN_HEADS, DN_HEAD_DIM, DN_HEAD_DIM), F32)],
        compiler_params=_params("parallel", "arbitrary"),
        name="gated_deltanet",
    )(pb, pb, ps, conv_w, alog_row, dt_row, onorm_row)


def _top2_of4(vals):
    m1 = jnp.maximum(jnp.maximum(vals[0], vals[1]), jnp.maximum(vals[2], vals[3]))
    i1 = jnp.where(vals[0] == m1, 0, jnp.where(vals[1] == m1, 1, jnp.where(vals[2] == m1, 2, 3)))
    rest = [jnp.where(i1 == r, -jnp.inf, vals[r]) for r in range(4)]
    m2 = jnp.maximum(jnp.maximum(rest[0], rest[1]), jnp.maximum(rest[2], rest[3]))
    i2 = jnp.where(rest[0] == m2, 0, jnp.where(rest[1] == m2, 1, jnp.where(rest[2] == m2, 2, 3)))
    return m1, i1, m2, i2


def _pick4(idx, vals):
    return jnp.where(idx == 0, vals[0], jnp.where(idx == 1, vals[1], jnp.where(idx == 2, vals[2], vals[3])))


def _merge_kernel(oa_ref, ob_ref, ga_ref, gb_ref, x_ref, wa_ref, wb_ref, wo_ref, g_ref, b_ref,
                  wr_ref, rb_ref, h_ref, ri_ref, rw_ref, cnt_ref, carry_ref, *, tm, alpha):
    t = pl.program_id(0)

    @pl.when(t == 0)
    def _():
        carry_ref[...] = jnp.zeros_like(carry_ref)

    ya = jnp.dot(oa_ref[...], wa_ref[...], preferred_element_type=F32)
    yb = jnp.dot(ob_ref[...], wb_ref[...], preferred_element_type=F32)
    merged = _sigmoid(ga_ref[...]) * ya + _sigmoid(gb_ref[...]) * yb
    mix = jnp.dot(merged.astype(BF16), wo_ref[...], preferred_element_type=F32)
    h = _layer_norm(alpha * x_ref[...] + mix, g_ref[...], b_ref[...])
    h_ref[...] = h

    logits = jnp.dot(h.astype(BF16), wr_ref[...], preferred_element_type=F32)
    lt = logits.T[:N_EXPERTS, :]
    ex = jnp.exp(lt - jnp.max(lt, axis=0, keepdims=True))
    probs = ex / jnp.sum(ex, axis=0, keepdims=True)
    sel = probs + rb_ref[...]
    sel_rows = [sel[e:e + 1, :] for e in range(N_EXPERTS)]
    prob_rows = [probs[e:e + 1, :] for e in range(N_EXPERTS)]
    scores = []
    for g in range(N_GROUPS):
        m1, _, m2, _ = _top2_of4(sel_rows[g * EXPERTS_PER_GROUP:(g + 1) * EXPERTS_PER_GROUP])
        scores.append(m1 + m2)
    best = jnp.maximum(jnp.maximum(scores[0], scores[1]), jnp.maximum(scores[2], scores[3]))
    grp = jnp.where(scores[0] == best, 0, jnp.where(scores[1] == best, 1, jnp.where(scores[2] == best, 2, 3)))
    in_grp = [_pick4(grp, [sel_rows[g * EXPERTS_PER_GROUP + r] for g in range(N_GROUPS)])
              for r in range(EXPERTS_PER_GROUP)]
    p_grp = [_pick4(grp, [prob_rows[g * EXPERTS_PER_GROUP + r] for g in range(N_GROUPS)])
             for r in range(EXPERTS_PER_GROUP)]
    _, l0, _, l1 = _top2_of4(in_grp)
    e0 = grp * EXPERTS_PER_GROUP + l0
    e1 = grp * EXPERTS_PER_GROUP + l1
    p0 = _pick4(l0, p_grp)
    p1 = _pick4(l1, p_grp)
    w0 = p0 / (p0 + p1)
    w1 = p1 / (p0 + p1)

    eidx = lax.broadcasted_iota(jnp.int32, (N_EXPERTS, tm), 0)
    hit0 = eidx == e0
    hit1 = eidx == e1
    onehot = jnp.where(hit0 | hit1, 1.0, 0.0)
    trow = lax.broadcasted_iota(jnp.int32, (tm, tm), 0)
    tcol = lax.broadcasted_iota(jnp.int32, (tm, tm), 1)
    earlier = jnp.where(trow < tcol, 1.0, 0.0).astype(BF16)
    before = jnp.dot(onehot.astype(BF16), earlier, preferred_element_type=F32) + carry_ref[...]
    rank0 = jnp.sum(jnp.where(hit0, before, 0.0), axis=0, keepdims=True).astype(jnp.int32)
    rank1 = jnp.sum(jnp.where(hit1, before, 0.0), axis=0, keepdims=True).astype(jnp.int32)
    carry = carry_ref[...] + jnp.sum(onehot, axis=1, keepdims=True)
    carry_ref[...] = carry
    cnt_ref[...] = jnp.broadcast_to(carry, cnt_ref.shape).astype(jnp.int32)

    r8 = lax.broadcasted_iota(jnp.int32, (8, tm), 0)
    ri_ref[...] = jnp.where(r8 == 0, e0, jnp.where(r8 == 1, e1, jnp.where(r8 == 2, rank0,
                            jnp.where(r8 == 3, rank1, 0))))
    r128 = lax.broadcasted_iota(jnp.int32, (LANES, tm), 0)
    rw_ref[...] = jnp.where(r128 == 0, w0, jnp.where(r128 == 1, w1, 0.0)).T


def _merge(o_a, o_b, gates, x, w_br_a, w_br_b, w_o, ln_g, ln_b, w_router, router_bias, alpha):
    n = x.shape[0]
    tm = MERGE_TM
    full = lambda shape: pl.BlockSpec(shape, lambda t: (0,) * len(shape))
    return pl.pallas_call(
        functools.partial(_merge_kernel, tm=tm, alpha=alpha),
        grid=(n // tm,),
        in_specs=[pl.BlockSpec((tm, SB_WIDTH), lambda t: (t, 0)),
                  pl.BlockSpec((tm, DN_WIDTH), lambda t: (t, 0)),
                  pl.BlockSpec((tm, D_MODEL), lambda t: (t, 0)),
                  pl.BlockSpec((tm, D_MODEL), lambda t: (t, 1)),
                  pl.BlockSpec((tm, D_MODEL), lambda t: (t, 0)),
                  full((SB_WIDTH, D_MODEL)), full((DN_WIDTH, D_MODEL)), full((D_MODEL, D_MODEL)),
                  full((1, D_MODEL)), full((1, D_MODEL)),
                  full((D_MODEL, LANES)), full((N_EXPERTS, 1))],
        out_specs=[pl.BlockSpec((tm, D_MODEL), lambda t: (t, 0)),
                   pl.BlockSpec((8, tm), lambda t: (0, t)),
                   pl.BlockSpec((tm, LANES), lambda t: (t, 0)),
                   pl.BlockSpec((N_EXPERTS, LANES), lambda t: (0, 0))],
        out_shape=[jax.ShapeDtypeStruct((n, D_MODEL), F32),
                   jax.ShapeDtypeStruct((8, n), jnp.int32),
                   jax.ShapeDtypeStruct((n, LANES), F32),
                   jax.ShapeDtypeStruct((N_EXPERTS, LANES), jnp.int32)],
        scratch_shapes=[pltpu.VMEM((N_EXPERTS, 1), F32)],
        compiler_params=_params("arbitrary"),
        name="merge_ln_route",
    )(o_a, o_b, gates, gates, x, w_br_a, w_br_b, w_o, ln_g, ln_b, w_router, router_bias)


def _row_copy(src_hbm, row, dst, r, sem):
    return pltpu.make_async_copy(src_hbm.at[pl.ds(row, 1), :], dst.at[pl.ds(r, 1), :], sem)


def _expert_kernel(blk_exp_ref, row_tok_ref, h_hbm, wg_ref, wu_ref, wd_ref, y_ref, xbuf, sem, *, rb):
    del blk_exp_ref
    i = pl.program_id(0)
    n = pl.num_programs(0)
    slot = i % 2

    def gather(block, dst_slot):
        base = block * rb

        def issue(r8, _):
            for u in range(8):
                r = r8 * 8 + u
                _row_copy(h_hbm, row_tok_ref[base + r], xbuf.at[dst_slot], r, sem.at[dst_slot]).start()
            return 0

        lax.fori_loop(0, rb // 8, issue, 0)

    @pl.when(i == 0)
    def _():
        gather(0, 0)

    @pl.when(i + 1 < n)
    def _():
        gather(i + 1, 1 - slot)

    pltpu.make_async_copy(h_hbm.at[pl.ds(0, rb), :], xbuf.at[slot], sem.at[slot]).wait()
    x = xbuf[slot].astype(BF16)
    gate = jnp.dot(x, wg_ref[...], preferred_element_type=F32)
    up = jnp.dot(x, wu_ref[...], preferred_element_type=F32)
    hid = (gate * _sigmoid(gate) * up).astype(BF16)
    y_ref[...] = jnp.dot(hid, wd_ref[...], preferred_element_type=F32)


def _experts(h, blk_exp, row_tok, w_gate, w_up, w_down):
    rb = ROUTE_BLOCK
    p_rows = row_tok.shape[0]
    wspec = lambda a, b: pl.BlockSpec((None, a, b), lambda i, be, rt: (be[i], 0, 0))
    return pl.pallas_call(
        functools.partial(_expert_kernel, rb=rb),
        grid_spec=pltpu.PrefetchScalarGridSpec(
            num_scalar_prefetch=2,
            grid=(p_rows // rb,),
            in_specs=[pl.BlockSpec(memory_space=pl.ANY),
                      wspec(D_MODEL, D_EXPERT), wspec(D_MODEL, D_EXPERT), wspec(D_EXPERT, D_MODEL)],
            out_specs=pl.BlockSpec((rb, D_MODEL), lambda i, be, rt: (i, 0)),
            scratch_shapes=[pltpu.VMEM((2, rb, D_MODEL), F32), pltpu.SemaphoreType.DMA((2,))]),
        out_shape=jax.ShapeDtypeStruct((p_rows, D_MODEL), F32),
        compiler_params=_params("arbitrary"),
        name="expert_ffn",
    )(blk_exp, row_tok, h, w_gate, w_up, w_down)


def _combine_kernel(d0_ref, d1_ref, h_ref, rw_ref, y_hbm, g_ref, b_ref, o_ref, ybuf, sem, *, tm, alpha):
    t = pl.program_id(0)
    n = pl.num_programs(0)
    slot = t % 2

    def gather(tile, dst_slot):
        base = tile * tm

        def issue(r8, _):
            for u in range(8):
                r = r8 * 8 + u
                _row_copy(y_hbm, d0_ref[base + r], ybuf.at[dst_slot, 0], r, sem.at[dst_slot]).start()
                _row_copy(y_hbm, d1_ref[base + r], ybuf.at[dst_slot, 1], r, sem.at[dst_slot]).start()
            return 0

        lax.fori_loop(0, tm // 8, issue, 0)

    @pl.when(t == 0)
    def _():
        gather(0, 0)

    @pl.when(t + 1 < n)
    def _():
        gather(t + 1, 1 - slot)

    for k in range(TOP_K):
        pltpu.make_async_copy(y_hbm.at[pl.ds(0, tm), :], ybuf.at[slot, k], sem.at[slot]).wait()
    rw = rw_ref[...]
    ffn = rw[:, 0:1] * ybuf[slot, 0] + rw[:, 1:2] * ybuf[slot, 1]
    o_ref[...] = _layer_norm(alpha * h_ref[...] + ffn, g_ref[...], b_ref[...])


def _combine(h, rw, ys, dest0, dest1, ln_g, ln_b, alpha):
    n = h.shape[0]
    tm = COMBINE_TM
    return pl.pallas_call(
        functools.partial(_combine_kernel, tm=tm, alpha=alpha),
        grid_spec=pltpu.PrefetchScalarGridSpec(
            num_scalar_prefetch=2,
            grid=(n // tm,),
            in_specs=[pl.BlockSpec((tm, D_MODEL), lambda t, a, b: (t, 0)),
                      pl.BlockSpec((tm, LANES), lambda t, a, b: (t, 0)),
                      pl.BlockSpec(memory_space=pl.ANY),
                      pl.BlockSpec((1, D_MODEL), lambda t, a, b: (0, 0)),
                      pl.BlockSpec((1, D_MODEL), lambda t, a, b: (0, 0))],
            out_specs=pl.BlockSpec((tm, D_MODEL), lambda t, a, b: (t, 0)),
            scratch_shapes=[pltpu.VMEM((2, TOP_K, tm, D_MODEL), F32), pltpu.SemaphoreType.DMA((2,))]),
        out_shape=jax.ShapeDtypeStruct((n, D_MODEL), F32),
        compiler_params=_params("arbitrary"),
        name="combine_ln",
    )(dest0, dest1, h, rw, ys, ln_g, ln_b)


def _route_rows(ri, cnt, n_tok):
    rb = ROUTE_BLOCK
    counts = cnt[:, 0]
    padded = (counts + rb - 1) // rb * rb
    pends = jnp.cumsum(padded)
    pstarts = pends - padded
    dest0 = pstarts[ri[0]] + ri[2]
    dest1 = pstarts[ri[1]] + ri[3]
    p_rows = n_tok * TOP_K + N_EXPERTS * rb
    tok = jnp.arange(n_tok, dtype=jnp.int32)
    row_tok = jnp.zeros((p_rows,), jnp.int32).at[dest0].set(tok).at[dest1].set(tok)
    blk_start = jnp.arange(p_rows // rb, dtype=jnp.int32) * rb
    blk_exp = jnp.minimum(jnp.searchsorted(pends, blk_start, side="right"), N_EXPERTS - 1).astype(jnp.int32)
    return dest0.astype(jnp.int32), dest1.astype(jnp.int32), row_tok, blk_exp


def _lane_row(values, offset):
    return jnp.zeros((1, LANES), F32).at[0, offset:offset + values.shape[0]].set(values.astype(F32))


def kernel(x, w_in, conv_w, a_log, dt_bias, onorm_g, w_br_a, w_br_b, w_o, ln1_g, ln1_b, w_router,
           router_bias, w_gate, w_up, w_down, ln2_g, ln2_b):
    batch, seq, d = x.shape
    depth = w_in.shape[0]
    n_tok = batch * seq
    alpha = (2 * depth) ** 0.25
    a_end = 3 * SB_WIDTH
    b_end = a_end + 4 * DN_WIDTH
    s_end = b_end + 2 * DN_HEADS
    w_a = w_in[:, :, :a_end].astype(BF16)
    w_b = w_in[:, :, a_end:b_end].astype(BF16)
    w_s = jnp.pad(w_in[:, :, b_end:s_end], ((0, 0), (0, 0), (0, LANES - 2 * DN_HEADS))).astype(BF16)
    w_g = w_in[:, :, s_end:].astype(BF16)
    w_r = jnp.pad(w_router, ((0, 0), (0, LANES - N_EXPERTS))).astype(BF16)
    r_bias = router_bias.astype(F32).reshape(N_EXPERTS, 1)
    w_br_a, w_br_b, w_o = w_br_a.astype(BF16), w_br_b.astype(BF16), w_o.astype(BF16)
    w_gate, w_up, w_down = w_gate.astype(BF16), w_up.astype(BF16), w_down.astype(BF16)

    xf = x.reshape(n_tok, d)
    for l in range(depth):
        pa = _matmul(xf, w_a[l], BF16, PROJ_TN).reshape(batch, seq, a_end)
        pb = _matmul(xf, w_b[l], F32, PROJ_TN).reshape(batch, seq, 4 * DN_WIDTH)
        ps = _matmul(xf, w_s[l], F32, LANES).reshape(batch, seq, LANES)
        gates = _matmul(xf, w_g[l], F32, PROJ_TN)
        o_a = _stick_breaking(pa, batch, seq).reshape(n_tok, SB_WIDTH)
        o_b = _gated_deltanet(pb, ps, conv_w[l].astype(F32), _lane_row(a_log[l], DN_HEADS),
                              _lane_row(dt_bias[l], DN_HEADS), onorm_g[l].astype(F32).reshape(1, DN_HEAD_DIM),
                              batch, seq).reshape(n_tok, DN_WIDTH)
        h, ri, rw, cnt = _merge(o_a, o_b, gates, xf, w_br_a[l], w_br_b[l], w_o[l],
                                ln1_g[l].reshape(1, d), ln1_b[l].reshape(1, d), w_r, r_bias, alpha)
        dest0, dest1, row_tok, blk_exp = _route_rows(ri, cnt, n_tok)
        ys = _experts(h, blk_exp, row_tok, w_gate[l], w_up[l], w_down[l])
        xf = _combine(h, rw, ys, dest0, dest1, ln2_g[l].reshape(1, d), ln2_b[l].reshape(1, d), alpha)
    return xf.reshape(batch, seq, d)
```

```python
import functools

import jax
import jax.numpy as jnp
from jax import lax
from jax.experimental import pallas as pl
from jax.experimental.pallas import tpu as pltpu

D_MODEL = 1024
SB_HEADS = 8
SB_HEAD_DIM = 64
SB_WIDTH = SB_HEADS * SB_HEAD_DIM
DN_HEADS = 4
DN_HEAD_DIM = 128
DN_WIDTH = DN_HEADS * DN_HEAD_DIM
CONV_K = 4
CHUNK = 64
N_EXPERTS = 16
N_GROUPS = 4
EXPERTS_PER_GROUP = N_EXPERTS // N_GROUPS
TOP_K = 2
D_EXPERT = 512
LN_EPS = 1e-5
NORM_EPS = 1e-6

LANES = 128
VMEM_LIMIT = 48 * 1024 * 1024

PROJ_TM = 1024
PROJ_TN = 512
SB_TQ = 128
SB_TK = 256
GDN_TS = 512
GDN_GROUP = 4
MERGE_TM = 256
ROUTE_BLOCK = 256
COMBINE_TM = 256
SB_SKIP_LOG = -100.0

BF16 = jnp.bfloat16
F32 = jnp.float32
NT_DIMS = (((1,), (1,)), ((), ()))
TN_DIMS = (((0,), (0,)), ((), ()))


def _params(*semantics):
    return pltpu.CompilerParams(dimension_semantics=semantics, vmem_limit_bytes=VMEM_LIMIT)


def _sigmoid(x):
    return 1.0 / (1.0 + jnp.exp(-x))


def _softplus(x):
    return jnp.maximum(x, 0.0) + jnp.log(1.0 + jnp.exp(-jnp.abs(x)))


def _layer_norm(y, g, b):
    mu = jnp.mean(y, axis=-1, keepdims=True)
    yc = y - mu
    var = jnp.mean(yc * yc, axis=-1, keepdims=True)
    return yc * lax.rsqrt(var + LN_EPS) * g + b


def _mm_kernel(x_ref, w_ref, o_ref):
    o_ref[...] = jnp.dot(x_ref[...], w_ref[...], preferred_element_type=F32).astype(o_ref.dtype)


def _matmul(x, w, out_dtype, tn):
    m, k = x.shape
    n = w.shape[1]
    return pl.pallas_call(
        _mm_kernel,
        grid=(m // PROJ_TM, n // tn),
        in_specs=[pl.BlockSpec((PROJ_TM, k), lambda i, j: (i, 0)),
                  pl.BlockSpec((k, tn), lambda i, j: (0, j))],
        out_specs=pl.BlockSpec((PROJ_TM, tn), lambda i, j: (i, j)),
        out_shape=jax.ShapeDtypeStruct((m, n), out_dtype),
        compiler_params=_params("parallel", "parallel"),
        name="proj",
    )(x, w)


def _sb_kernel(q_ref, k_ref, v_ref, later_ref, o_ref, acc_ref, c_ref, *, tq, tk):
    i = pl.program_id(2)
    lane = lax.broadcasted_iota(jnp.int32, (1, LANES), 1)
    head_a = lane < SB_HEAD_DIM
    q = q_ref[...] * jnp.asarray(SB_HEAD_DIM ** -0.5, q_ref.dtype)
    zero = jnp.zeros_like(q)
    q_heads = (jnp.where(head_a, q, zero), jnp.where(head_a, zero, q))
    acc_ref[...] = jnp.zeros_like(acc_ref)
    c_ref[...] = jnp.zeros_like(c_ref)
    q_end = (i + 1) * tq
    query = i * tq + lax.broadcasted_iota(jnp.int32, (tq, tk), 0)
    col = lax.broadcasted_iota(jnp.int32, (tq, tk), 1)
    later = later_ref[...]

    def body(carry):
        b, _ = carry
        nominal = q_end - (b + 1) * tk
        start = pl.multiple_of(jnp.maximum(nominal, 0), tq)
        k = k_ref[pl.ds(start, tk), :]
        v = v_ref[pl.ds(start, tk), :]
        key = start + col
        valid = (key < query) & (key < nominal + tk)
        cmax = None
        for h in range(2):
            z = lax.dot_general(q_heads[h], k, NT_DIMS, preferred_element_type=F32)
            log_beta = jnp.minimum(z, 0.0) - jnp.log(1.0 + jnp.exp(-jnp.abs(z)))
            log_keep = jnp.where(valid, log_beta - z, 0.0)
            sums = jnp.dot(log_keep.astype(BF16), later, preferred_element_type=F32)
            a = jnp.where(valid, jnp.exp(log_beta + sums[:, :tk]), 0.0)
            pv = jnp.dot(a.astype(BF16), v, preferred_element_type=F32)
            c = c_ref[h]
            acc_ref[h] += jnp.exp(c) * pv
            c = c + sums[:, tk:]
            c_ref[h] = c
            m = jnp.max(c)
            cmax = m if cmax is None else jnp.maximum(cmax, m)
        return b + 1, cmax

    lax.while_loop(lambda cr: (q_end - cr[0] * tk > 0) & (cr[1] > SB_SKIP_LOG), body,
                   (jnp.int32(0), jnp.float32(0.0)))
    o_ref[...] = jnp.where(head_a, acc_ref[0], acc_ref[1]).astype(o_ref.dtype)


def _stick_breaking(qkv, batch, seq):
    pairs = SB_WIDTH // LANES
    tq, tk = SB_TQ, SB_TK
    r = jnp.arange(tk)[:, None]
    c = jnp.arange(tk + LANES)[None, :]
    later = ((c >= tk) | (r > c)).astype(BF16)
    return pl.pallas_call(
        functools.partial(_sb_kernel, tq=tq, tk=tk),
        grid=(batch, pairs, seq // tq),
        in_specs=[pl.BlockSpec((None, tq, LANES), lambda b, p, i: (b, i, p)),
                  pl.BlockSpec((None, seq, LANES), lambda b, p, i: (b, 0, pairs + p)),
                  pl.BlockSpec((None, seq, LANES), lambda b, p, i: (b, 0, 2 * pairs + p)),
                  pl.BlockSpec((tk, tk + LANES), lambda b, p, i: (0, 0))],
        out_specs=pl.BlockSpec((None, tq, LANES), lambda b, p, i: (b, i, p)),
        out_shape=jax.ShapeDtypeStruct((batch, seq, SB_WIDTH), BF16),
        scratch_shapes=[pltpu.VMEM((2, tq, LANES), F32), pltpu.VMEM((2, tq, LANES), F32)],
        compiler_params=_params("parallel", "parallel", "parallel"),
        name="stick_breaking",
    )(qkv, qkv, qkv, later)


CAT = DN_HEADS * CHUNK


def _split3(x):
    hi = x.astype(BF16)
    r = x - hi.astype(F32)
    mid = r.astype(BF16)
    return hi, mid, (r - mid.astype(F32)).astype(BF16)


def _block_diag(x, mask):
    return jnp.concatenate([x] * DN_HEADS, axis=0) * mask


def _cat_matmul(l, r, mask):
    return jnp.dot(l.astype(BF16), _block_diag(r.astype(BF16), mask), preferred_element_type=F32)


def _by_head(head, vals):
    return jnp.where(head == 0, vals[0], jnp.where(head == 1, vals[1], jnp.where(head == 2, vals[2], vals[3])))


def _gdn_constants():
    i = jnp.arange(CHUNK)
    tri = (i[:, None] >= i[None, :]).astype(BF16)
    row_head = jnp.arange(CAT)[:, None] // CHUNK
    wide_head = jnp.arange(DN_WIDTH) // DN_HEAD_DIM
    return (jnp.concatenate([tri] * 3, axis=1),
            (row_head == jnp.arange(CAT)[None, :] // CHUNK).astype(BF16),
            (row_head == wide_head[None, :]).astype(BF16),
            (row_head == jnp.concatenate([wide_head, wide_head])[None, :]).astype(BF16))


def _gdn_kernel(qkv_ref, z_ref, s_ref, cw_ref, alog_ref, dt_ref, og_ref, tri3_ref, bd_ref, bdk_ref,
                bdr_ref, o_ref, ubuf, cbuf, state_ref, sbd_ref, *, ts):
    si = pl.program_id(1)
    tail = 8

    @pl.when(si == 0)
    def _():
        state_ref[...] = jnp.zeros_like(state_ref)
        sbd_ref[...] = jnp.zeros_like(sbd_ref)
        ubuf[0:tail, :] = jnp.zeros((tail, 3 * DN_WIDTH), F32)

    @pl.when(si > 0)
    def _():
        ubuf[0:tail, :] = ubuf[ts:ts + tail, :]

    ubuf[tail:tail + ts, :] = qkv_ref[...]
    conv = cw_ref[0:1, :] * ubuf[tail - 3:tail - 3 + ts, :]
    for j in range(1, CONV_K):
        conv = conv + cw_ref[j:j + 1, :] * ubuf[tail - 3 + j:tail - 3 + j + ts, :]
    cbuf[...] = conv * _sigmoid(conv)

    rows = lax.broadcasted_iota(jnp.int32, (CHUNK, CAT), 0)
    lane = lax.broadcasted_iota(jnp.int32, (CHUNK, CAT), 1)
    cols = lane % CHUNK
    head = lane // CHUNK
    incl = rows >= cols
    strict = rows > cols
    upper = rows <= cols
    pair = rows // 2 == cols // 2
    lower_blocks = []
    blk = 2
    while blk < CHUNK:
        lower_blocks.append((rows // (2 * blk) == cols // (2 * blk)) & (rows // blk != cols // blk))
        blk *= 2
    neg_a = -jnp.exp(alog_ref[...])
    dt_bias = dt_ref[...]
    onorm_g = og_ref[...]

    heads = range(DN_HEADS)

    def operands(r0, beta_all, gcol):
        qs, ks, q_decs, k_decs, rhs_v, rhs_k, lasts = [], [], [], [], [], [], []
        for h in heads:
            lo, hi = h * DN_HEAD_DIM, (h + 1) * DN_HEAD_DIM
            q = cbuf[pl.ds(r0, CHUNK), lo:hi]
            k = cbuf[pl.ds(r0, CHUNK), DN_WIDTH + lo:DN_WIDTH + hi]
            v = cbuf[pl.ds(r0, CHUNK), 2 * DN_WIDTH + lo:2 * DN_WIDTH + hi]
            q = q * lax.rsqrt(jnp.sum(q * q, axis=-1, keepdims=True) + NORM_EPS) * (DN_HEAD_DIM ** -0.5)
            k = k * lax.rsqrt(jnp.sum(k * k, axis=-1, keepdims=True) + NORM_EPS)
            beta = beta_all[:, h:h + 1]
            gc = gcol[:, h * CHUNK:h * CHUNK + 1]
            glast = gc[CHUNK - 1:CHUNK, :]
            eg = jnp.exp(gc)
            qs.append(q)
            ks.append(k)
            q_decs.append(q * eg)
            k_decs.append((k * jnp.exp(glast - gc)).astype(BF16))
            rhs_v.append(beta * v)
            rhs_k.append(beta * eg * k)
            lasts.append(jnp.exp(glast))
        return (jnp.concatenate(ks, axis=1).astype(BF16), jnp.concatenate(qs, axis=1).astype(BF16),
                jnp.concatenate(rhs_v + rhs_k, axis=1), jnp.concatenate(q_decs, axis=1).astype(BF16),
                k_decs, lasts)

    def group_body(gi, _):
        group = range(GDN_GROUP)
        r0s = [pl.multiple_of((gi * GDN_GROUP + j) * CHUNK, CHUNK) for j in group]
        sms = [s_ref[pl.ds(r0, CHUNK), :] for r0 in r0s]
        beta_alls = [_sigmoid(sm) for sm in sms]
        g_alls = [neg_a * _softplus(sm + dt_bias) for sm in sms]
        beta_cats = [_by_head(head, [b[:, h:h + 1] for h in heads]) for b in beta_alls]
        g_cats = [_by_head(head, [g[:, DN_HEADS + h:DN_HEADS + h + 1] for h in heads]) for g in g_alls]
        gcols = [jnp.dot(tri3_ref[...], jnp.concatenate(_split3(g), axis=0), preferred_element_type=F32)
                 for g in g_cats]
        grows = [jnp.sum(jnp.where(upper, g, 0.0), axis=0, keepdims=True) for g in g_cats]
        gammas = [jnp.exp(jnp.where(incl, gc - gr, -jnp.inf)) for gc, gr in zip(gcols, grows)]
        ops = [operands(r0, b, gc) for r0, b, gc in zip(r0s, beta_alls, gcols)]
        bdk = bdk_ref[...]
        kqs = [lax.dot_general(jnp.concatenate([kc, qc], axis=0), _block_diag(kc, bdk), NT_DIMS,
                               preferred_element_type=F32) for kc, qc, _, _, _, _ in ops]
        qks = [(kq[CHUNK:] * gm).astype(BF16) for kq, gm in zip(kqs, gammas)]
        a_lows = [jnp.where(strict, b * kq[:CHUNK] * gm, 0.0) for b, kq, gm in zip(beta_cats, kqs, gammas)]
        bd = bd_ref[...]
        ys = [-jnp.where(pair, a, 0.0) for a in a_lows]
        for lower in lower_blocks:
            lows = [jnp.where(lower, a, 0.0) for a in a_lows]
            ts_ = [low + _cat_matmul(low, y, bd) for low, y in zip(lows, ys)]
            ys = [y - t - _cat_matmul(y, t, bd) for y, t in zip(ys, ts_)]
        bdr = bdr_ref[...]
        uws = [op[2] + _cat_matmul(y, op[2], bdr) for y, op in zip(ys, ops)]
        for j in group:
            _, _, _, q_dec, k_decs, lasts = ops[j]
            u, w = uws[j][:, :DN_WIDTH], uws[j][:, DN_WIDTH:]
            ws = jnp.dot(jnp.concatenate([w.astype(BF16), q_dec], axis=0), sbd_ref[...],
                         preferred_element_type=F32)
            vb = (u - ws[:CHUNK]).astype(BF16)
            o = ws[CHUNK:] + jnp.dot(qks[j], _block_diag(vb, bdk), preferred_element_type=F32)
            for h in heads:
                lo, hi = h * DN_HEAD_DIM, (h + 1) * DN_HEAD_DIM
                new = lasts[h] * state_ref[h] + lax.dot_general(k_decs[h], vb[:, lo:hi], TN_DIMS,
                                                                preferred_element_type=F32)
                state_ref[h] = new
                sbd_ref[lo:hi, lo:hi] = new.astype(BF16)
                oh = o[:, lo:hi]
                oh = oh * lax.rsqrt(jnp.mean(oh * oh, axis=-1, keepdims=True) + NORM_EPS) * onorm_g
                zz = z_ref[pl.ds(r0s[j], CHUNK), lo:hi]
                o_ref[pl.ds(r0s[j], CHUNK), lo:hi] = (oh * (zz * _sigmoid(zz))).astype(o_ref.dtype)
        return 0

    lax.fori_loop(0, ts // (CHUNK * GDN_GROUP), group_body, 0)


def _gated_deltanet(pb, ps, conv_w, alog_row, dt_row, onorm_row, batch, seq):
    ts = GDN_TS
    qkv_blocks = 3 * DN_WIDTH // DN_WIDTH
    consts = _gdn_constants()
    const = lambda a: pl.BlockSpec(a.shape, lambda b, s: (0, 0))
    return pl.pallas_call(
        functools.partial(_gdn_kernel, ts=ts),
        grid=(batch, seq // ts),
        in_specs=[pl.BlockSpec((None, ts, 3 * DN_WIDTH), lambda b, s: (b, s, 0)),
                  pl.BlockSpec((None, ts, DN_WIDTH), lambda b, s: (b, s, qkv_blocks)),
                  pl.BlockSpec((None, ts, LANES), lambda b, s: (b, s, 0)),
                  const(conv_w), const(alog_row), const(dt_row), const(onorm_row)]
                 + [const(a) for a in consts],
        out_specs=pl.BlockSpec((None, ts, DN_WIDTH), lambda b, s: (b, s, 0)),
        out_shape=jax.ShapeDtypeStruct((batch, seq, DN_WIDTH), BF16),
        scratch_shapes=[pltpu.VMEM((ts + 8, 3 * DN_WIDTH), F32),
                        pltpu.VMEM((ts, 3 * DN_WIDTH), F32),
                        pltpu.VMEM((DN_HEADS, DN_HEAD_DIM, DN_HEAD_DIM), F32),
                        pltpu.VMEM((DN_WIDTH, DN_WIDTH), BF16)],
        compiler_params=_params("parallel", "arbitrary"),
        name="gated_deltanet",
    )(pb, pb, ps, conv_w, alog_row, dt_row, onorm_row, *consts)


def _top2_of4(vals):
    m1 = jnp.maximum(jnp.maximum(vals[0], vals[1]), jnp.maximum(vals[2], vals[3]))
    i1 = jnp.where(vals[0] == m1, 0, jnp.where(vals[1] == m1, 1, jnp.where(vals[2] == m1, 2, 3)))
    rest = [jnp.where(i1 == r, -jnp.inf, vals[r]) for r in range(4)]
    m2 = jnp.maximum(jnp.maximum(rest[0], rest[1]), jnp.maximum(rest[2], rest[3]))
    i2 = jnp.where(rest[0] == m2, 0, jnp.where(rest[1] == m2, 1, jnp.where(rest[2] == m2, 2, 3)))
    return m1, i1, m2, i2


def _pick4(idx, vals):
    return jnp.where(idx == 0, vals[0], jnp.where(idx == 1, vals[1], jnp.where(idx == 2, vals[2], vals[3])))


def _merge_kernel(oa_ref, ob_ref, ga_ref, gb_ref, x_ref, wa_ref, wb_ref, wo_ref, g_ref, b_ref,
                  wr_ref, rb_ref, h_ref, ri_ref, rw_ref, cnt_ref, carry_ref, *, tm, alpha):
    t = pl.program_id(0)

    @pl.when(t == 0)
    def _():
        carry_ref[...] = jnp.zeros_like(carry_ref)

    ya = jnp.dot(oa_ref[...], wa_ref[...], preferred_element_type=F32)
    yb = jnp.dot(ob_ref[...], wb_ref[...], preferred_element_type=F32)
    merged = _sigmoid(ga_ref[...]) * ya + _sigmoid(gb_ref[...]) * yb
    mix = jnp.dot(merged.astype(BF16), wo_ref[...], preferred_element_type=F32)
    h = _layer_norm(alpha * x_ref[...] + mix, g_ref[...], b_ref[...])
    h_ref[...] = h

    logits = jnp.dot(h.astype(BF16), wr_ref[...], preferred_element_type=F32)
    lt = logits.T[:N_EXPERTS, :]
    ex = jnp.exp(lt - jnp.max(lt, axis=0, keepdims=True))
    probs = ex / jnp.sum(ex, axis=0, keepdims=True)
    sel = probs + rb_ref[...]
    sel_rows = [sel[e:e + 1, :] for e in range(N_EXPERTS)]
    prob_rows = [probs[e:e + 1, :] for e in range(N_EXPERTS)]
    scores = []
    for g in range(N_GROUPS):
        m1, _, m2, _ = _top2_of4(sel_rows[g * EXPERTS_PER_GROUP:(g + 1) * EXPERTS_PER_GROUP])
        scores.append(m1 + m2)
    best = jnp.maximum(jnp.maximum(scores[0], scores[1]), jnp.maximum(scores[2], scores[3]))
    grp = jnp.where(scores[0] == best, 0, jnp.where(scores[1] == best, 1, jnp.where(scores[2] == best, 2, 3)))
    in_grp = [_pick4(grp, [sel_rows[g * EXPERTS_PER_GROUP + r] for g in range(N_GROUPS)])
              for r in range(EXPERTS_PER_GROUP)]
    p_grp = [_pick4(grp, [prob_rows[g * EXPERTS_PER_GROUP + r] for g in range(N_GROUPS)])
             for r in range(EXPERTS_PER_GROUP)]
    _, l0, _, l1 = _top2_of4(in_grp)
    e0 = grp * EXPERTS_PER_GROUP + l0
    e1 = grp * EXPERTS_PER_GROUP + l1
    p0 = _pick4(l0, p_grp)
    p1 = _pick4(l1, p_grp)
    w0 = p0 / (p0 + p1)
    w1 = p1 / (p0 + p1)

    eidx = lax.broadcasted_iota(jnp.int32, (N_EXPERTS, tm), 0)
    hit0 = eidx == e0
    hit1 = eidx == e1
    onehot = jnp.where(hit0 | hit1, 1.0, 0.0)
    trow = lax.broadcasted_iota(jnp.int32, (tm, tm), 0)
    tcol = lax.broadcasted_iota(jnp.int32, (tm, tm), 1)
    earlier = jnp.where(trow < tcol, 1.0, 0.0).astype(BF16)
    before = jnp.dot(onehot.astype(BF16), earlier, preferred_element_type=F32) + carry_ref[...]
    rank0 = jnp.sum(jnp.where(hit0, before, 0.0), axis=0, keepdims=True).astype(jnp.int32)
    rank1 = jnp.sum(jnp.where(hit1, before, 0.0), axis=0, keepdims=True).astype(jnp.int32)
    carry = carry_ref[...] + jnp.sum(onehot, axis=1, keepdims=True)
    carry_ref[...] = carry
    cnt_ref[...] = jnp.broadcast_to(carry, cnt_ref.shape).astype(jnp.int32)

    r8 = lax.broadcasted_iota(jnp.int32, (8, tm), 0)
    ri_ref[...] = jnp.where(r8 == 0, e0, jnp.where(r8 == 1, e1, jnp.where(r8 == 2, rank0,
                            jnp.where(r8 == 3, rank1, 0))))
    r128 = lax.broadcasted_iota(jnp.int32, (LANES, tm), 0)
    rw_ref[...] = jnp.where(r128 == 0, w0, jnp.where(r128 == 1, w1, 0.0)).T


def _merge(o_a, o_b, gates, x, w_br_a, w_br_b, w_o, ln_g, ln_b, w_router, router_bias, alpha):
    n = x.shape[0]
    tm = MERGE_TM
    full = lambda shape: pl.BlockSpec(shape, lambda t: (0,) * len(shape))
    return pl.pallas_call(
        functools.partial(_merge_kernel, tm=tm, alpha=alpha),
        grid=(n // tm,),
        in_specs=[pl.BlockSpec((tm, SB_WIDTH), lambda t: (t, 0)),
                  pl.BlockSpec((tm, DN_WIDTH), lambda t: (t, 0)),
                  pl.BlockSpec((tm, D_MODEL), lambda t: (t, 0)),
                  pl.BlockSpec((tm, D_MODEL), lambda t: (t, 1)),
                  pl.BlockSpec((tm, D_MODEL), lambda t: (t, 0)),
                  full((SB_WIDTH, D_MODEL)), full((DN_WIDTH, D_MODEL)), full((D_MODEL, D_MODEL)),
                  full((1, D_MODEL)), full((1, D_MODEL)),
                  full((D_MODEL, LANES)), full((N_EXPERTS, 1))],
        out_specs=[pl.BlockSpec((tm, D_MODEL), lambda t: (t, 0)),
                   pl.BlockSpec((8, tm), lambda t: (0, t)),
                   pl.BlockSpec((tm, LANES), lambda t: (t, 0)),
                   pl.BlockSpec((N_EXPERTS, LANES), lambda t: (0, 0))],
        out_shape=[jax.ShapeDtypeStruct((n, D_MODEL), F32),
                   jax.ShapeDtypeStruct((8, n), jnp.int32),
                   jax.ShapeDtypeStruct((n, LANES), F32),
                   jax.ShapeDtypeStruct((N_EXPERTS, LANES), jnp.int32)],
        scratch_shapes=[pltpu.VMEM((N_EXPERTS, 1), F32)],
        compiler_params=_params("arbitrary"),
        name="merge_ln_route",
    )(o_a, o_b, gates, gates, x, w_br_a, w_br_b, w_o, ln_g, ln_b, w_router, router_bias)


def _row_copy(src_hbm, row, dst, r, sem):
    return pltpu.make_async_copy(src_hbm.at[pl.ds(row, 1), :], dst.at[pl.ds(r, 1), :], sem)


def _expert_kernel(blk_exp_ref, row_tok_ref, h_hbm, wg_ref, wu_ref, wd_ref, y_ref, xbuf, sem, *, rb):
    del blk_exp_ref
    i = pl.program_id(0)
    n = pl.num_programs(0)
    slot = i % 2

    def gather(block, dst_slot):
        base = block * rb

        def issue(r8, _):
            for u in range(8):
                r = r8 * 8 + u
                _row_copy(h_hbm, row_tok_ref[base + r], xbuf.at[dst_slot], r, sem.at[dst_slot]).start()
            return 0

        lax.fori_loop(0, rb // 8, issue, 0)

    @pl.when(i == 0)
    def _():
        gather(0, 0)

    @pl.when(i + 1 < n)
    def _():
        gather(i + 1, 1 - slot)

    pltpu.make_async_copy(h_hbm.at[pl.ds(0, rb), :], xbuf.at[slot], sem.at[slot]).wait()
    x = xbuf[slot].astype(BF16)
    gate = jnp.dot(x, wg_ref[...], preferred_element_type=F32)
    up = jnp.dot(x, wu_ref[...], preferred_element_type=F32)
    hid = (gate * _sigmoid(gate) * up).astype(BF16)
    y_ref[...] = jnp.dot(hid, wd_ref[...], preferred_element_type=F32)


def _experts(h, blk_exp, row_tok, w_gate, w_up, w_down):
    rb = ROUTE_BLOCK
    p_rows = row_tok.shape[0]
    wspec = lambda a, b: pl.BlockSpec((None, a, b), lambda i, be, rt: (be[i], 0, 0))
    return pl.pallas_call(
        functools.partial(_expert_kernel, rb=rb),
        grid_spec=pltpu.PrefetchScalarGridSpec(
            num_scalar_prefetch=2,
            grid=(p_rows // rb,),
            in_specs=[pl.BlockSpec(memory_space=pl.ANY),
                      wspec(D_MODEL, D_EXPERT), wspec(D_MODEL, D_EXPERT), wspec(D_EXPERT, D_MODEL)],
            out_specs=pl.BlockSpec((rb, D_MODEL), lambda i, be, rt: (i, 0)),
            scratch_shapes=[pltpu.VMEM((2, rb, D_MODEL), F32), pltpu.SemaphoreType.DMA((2,))]),
        out_shape=jax.ShapeDtypeStruct((p_rows, D_MODEL), F32),
        compiler_params=_params("arbitrary"),
        name="expert_ffn",
    )(blk_exp, row_tok, h, w_gate, w_up, w_down)


def _combine_kernel(d0_ref, d1_ref, h_ref, rw_ref, y_hbm, g_ref, b_ref, o_ref, ob_ref, ybuf, sem, *, tm, alpha):
    t = pl.program_id(0)
    n = pl.num_programs(0)
    slot = t % 2

    def gather(tile, dst_slot):
        base = tile * tm

        def issue(r8, _):
            for u in range(8):
                r = r8 * 8 + u
                _row_copy(y_hbm, d0_ref[base + r], ybuf.at[dst_slot, 0], r, sem.at[dst_slot]).start()
                _row_copy(y_hbm, d1_ref[base + r], ybuf.at[dst_slot, 1], r, sem.at[dst_slot]).start()
            return 0

        lax.fori_loop(0, tm // 8, issue, 0)

    @pl.when(t == 0)
    def _():
        gather(0, 0)

    @pl.when(t + 1 < n)
    def _():
        gather(t + 1, 1 - slot)

    for k in range(TOP_K):
        pltpu.make_async_copy(y_hbm.at[pl.ds(0, tm), :], ybuf.at[slot, k], sem.at[slot]).wait()
    rw = rw_ref[...]
    ffn = rw[:, 0:1] * ybuf[slot, 0] + rw[:, 1:2] * ybuf[slot, 1]
    out = _layer_norm(alpha * h_ref[...] + ffn, g_ref[...], b_ref[...])
    o_ref[...] = out
    ob_ref[...] = out.astype(BF16)


def _combine(h, rw, ys, dest0, dest1, ln_g, ln_b, alpha):
    n = h.shape[0]
    tm = COMBINE_TM
    return pl.pallas_call(
        functools.partial(_combine_kernel, tm=tm, alpha=alpha),
        grid_spec=pltpu.PrefetchScalarGridSpec(
            num_scalar_prefetch=2,
            grid=(n // tm,),
            in_specs=[pl.BlockSpec((tm, D_MODEL), lambda t, a, b: (t, 0)),
                      pl.BlockSpec((tm, LANES), lambda t, a, b: (t, 0)),
                      pl.BlockSpec(memory_space=pl.ANY),
                      pl.BlockSpec((1, D_MODEL), lambda t, a, b: (0, 0)),
                      pl.BlockSpec((1, D_MODEL), lambda t, a, b: (0, 0))],
            out_specs=[pl.BlockSpec((tm, D_MODEL), lambda t, a, b: (t, 0)),
                       pl.BlockSpec((tm, D_MODEL), lambda t, a, b: (t, 0))],
            scratch_shapes=[pltpu.VMEM((2, TOP_K, tm, D_MODEL), F32), pltpu.SemaphoreType.DMA((2,))]),
        out_shape=[jax.ShapeDtypeStruct((n, D_MODEL), F32), jax.ShapeDtypeStruct((n, D_MODEL), BF16)],
        compiler_params=_params("arbitrary"),
        name="combine_ln",
    )(dest0, dest1, h, rw, ys, ln_g, ln_b)


def _route_rows(ri, cnt, n_tok):
    rb = ROUTE_BLOCK
    counts = cnt[:, 0]
    padded = (counts + rb - 1) // rb * rb
    pends = jnp.cumsum(padded)
    pstarts = pends - padded
    dest0 = pstarts[ri[0]] + ri[2]
    dest1 = pstarts[ri[1]] + ri[3]
    p_rows = n_tok * TOP_K + N_EXPERTS * rb
    tok = jnp.arange(n_tok, dtype=jnp.int32)
    row_tok = jnp.zeros((p_rows,), jnp.int32).at[dest0].set(tok).at[dest1].set(tok)
    blk_start = jnp.arange(p_rows // rb, dtype=jnp.int32) * rb
    blk_exp = jnp.minimum(jnp.searchsorted(pends, blk_start, side="right"), N_EXPERTS - 1).astype(jnp.int32)
    return dest0.astype(jnp.int32), dest1.astype(jnp.int32), row_tok, blk_exp


def _lane_row(values, offset):
    return jnp.zeros((1, LANES), F32).at[0, offset:offset + values.shape[0]].set(values.astype(F32))


def kernel(x, w_in, conv_w, a_log, dt_bias, onorm_g, w_br_a, w_br_b, w_o, ln1_g, ln1_b, w_router,
           router_bias, w_gate, w_up, w_down, ln2_g, ln2_b):
    batch, seq, d = x.shape
    depth = w_in.shape[0]
    n_tok = batch * seq
    alpha = (2 * depth) ** 0.25
    a_end = 3 * SB_WIDTH
    b_end = a_end + 4 * DN_WIDTH
    s_end = b_end + 2 * DN_HEADS
    w_a = w_in[:, :, :a_end].astype(BF16)
    w_b = w_in[:, :, a_end:b_end].astype(BF16)
    w_s = jnp.pad(w_in[:, :, b_end:s_end], ((0, 0), (0, 0), (0, LANES - 2 * DN_HEADS))).astype(BF16)
    w_g = w_in[:, :, s_end:].astype(BF16)
    w_r = jnp.pad(w_router, ((0, 0), (0, LANES - N_EXPERTS))).astype(BF16)
    r_bias = router_bias.astype(F32).reshape(N_EXPERTS, 1)
    w_br_a, w_br_b, w_o = w_br_a.astype(BF16), w_br_b.astype(BF16), w_o.astype(BF16)
    w_gate, w_up, w_down = w_gate.astype(BF16), w_up.astype(BF16), w_down.astype(BF16)

    xf = x.reshape(n_tok, d)
    xb = xf.astype(BF16)
    for l in range(depth):
        pa = _matmul(xb, w_a[l], BF16, PROJ_TN).reshape(batch, seq, a_end)
        pb = _matmul(xb, w_b[l], F32, PROJ_TN).reshape(batch, seq, 4 * DN_WIDTH)
        ps = _matmul(xb, w_s[l], F32, LANES).reshape(batch, seq, LANES)
        gates = _matmul(xb, w_g[l], F32, PROJ_TN)
        o_a = _stick_breaking(pa, batch, seq).reshape(n_tok, SB_WIDTH)
        o_b = _gated_deltanet(pb, ps, conv_w[l].astype(F32), _lane_row(a_log[l], DN_HEADS),
                              _lane_row(dt_bias[l], DN_HEADS), onorm_g[l].astype(F32).reshape(1, DN_HEAD_DIM),
                              batch, seq).reshape(n_tok, DN_WIDTH)
        h, ri, rw, cnt = _merge(o_a, o_b, gates, xf, w_br_a[l], w_br_b[l], w_o[l],
                                ln1_g[l].reshape(1, d), ln1_b[l].reshape(1, d), w_r, r_bias, alpha)
        dest0, dest1, row_tok, blk_exp = _route_rows(ri, cnt, n_tok)
        ys = _experts(h, blk_exp, row_tok, w_gate[l], w_up[l], w_down[l])
        xf, xb = _combine(h, rw, ys, dest0, dest1, ln2_g[l].reshape(1, d), ln2_b[l].reshape(1, d), alpha)
    return xf.reshape(batch, seq, d)
```

```python
import functools

import jax
import jax.numpy as jnp
from jax import lax
from jax.experimental import pallas as pl
from jax.experimental.pallas import tpu as pltpu

D_MODEL = 1024
SB_HEADS = 8
SB_HEAD_DIM = 64
SB_WIDTH = SB_HEADS * SB_HEAD_DIM
DN_HEADS = 4
DN_HEAD_DIM = 128
DN_WIDTH = DN_HEADS * DN_HEAD_DIM
CONV_K = 4
CHUNK = 64
N_EXPERTS = 16
N_GROUPS = 4
EXPERTS_PER_GROUP = N_EXPERTS // N_GROUPS
D_EXPERT = 512
LN_EPS = 1e-5
NORM_EPS = 1e-6

LANES = 128
H_EXT = D_MODEL + LANES
VMEM_LIMIT = 48 * 1024 * 1024

PROJ_TM = 1024
PROJ_TN = 512
SB_TQ = 128
SB_TK = 256
SB_SUB = 4
GDN_TS = 512
GDN_GROUP = 8
MERGE_TM = 256
ROUTE_BLOCK = 256
NORM_TM = 512
SB_SKIP_LOG = -60.0

BF16 = jnp.bfloat16
F32 = jnp.float32
NT_DIMS = (((1,), (1,)), ((), ()))
TN_DIMS = (((0,), (0,)), ((), ()))


def _params(*semantics):
    return pltpu.CompilerParams(dimension_semantics=semantics, vmem_limit_bytes=VMEM_LIMIT)


def _sigmoid(x):
    return 1.0 / (1.0 + jnp.exp(-x))


def _softplus(x):
    return jnp.maximum(x, 0.0) + jnp.log(1.0 + jnp.exp(-jnp.abs(x)))


def _layer_norm(y, g, b):
    mu = jnp.mean(y, axis=-1, keepdims=True)
    yc = y - mu
    var = jnp.mean(yc * yc, axis=-1, keepdims=True)
    return yc * lax.rsqrt(var + LN_EPS) * g + b


def _mm_kernel(x_ref, w_ref, o_ref):
    o_ref[...] = jnp.dot(x_ref[...], w_ref[...], preferred_element_type=F32).astype(o_ref.dtype)


def _matmul(x, w, out_dtype, tn):
    m, k = x.shape
    n = w.shape[1]
    return pl.pallas_call(
        _mm_kernel,
        grid=(m // PROJ_TM, n // tn),
        in_specs=[pl.BlockSpec((PROJ_TM, k), lambda i, j: (i, 0)),
                  pl.BlockSpec((k, tn), lambda i, j: (0, j))],
        out_specs=pl.BlockSpec((PROJ_TM, tn), lambda i, j: (i, j)),
        out_shape=jax.ShapeDtypeStruct((m, n), out_dtype),
        compiler_params=_params("parallel", "parallel"),
        name="proj",
    )(x, w)


def _sb_kernel(q_ref, k_ref, v_ref, later_ref, o_ref, acc_ref, c_ref, *, tq, tk, sub):
    i = pl.program_id(2)
    lane = lax.broadcasted_iota(jnp.int32, (1, LANES), 1)
    head_a = lane < SB_HEAD_DIM
    row = lax.broadcasted_iota(jnp.int32, (tq, tk), 0)
    col = lax.broadcasted_iota(jnp.int32, (tq, tk), 1)
    later = later_ref[...]
    tiles = range(sub)
    pairs = [(s, h) for s in tiles for h in range(2)]
    q_ends = [(i * sub + s + 1) * tq for s in tiles]
    q_heads = []
    for s in tiles:
        q = q_ref[s * tq:(s + 1) * tq, :] * jnp.asarray(SB_HEAD_DIM ** -0.5, q_ref.dtype)
        zero = jnp.zeros_like(q)
        q_heads.append((jnp.where(head_a, q, zero), jnp.where(head_a, zero, q)))

    def walk(b, first):
        nominal = [qe - (b + 1) * tk for qe in q_ends]
        starts = [pl.multiple_of(jnp.maximum(nm, 0), tq) for nm in nominal]
        ks = [k_ref[pl.ds(st, tk), :] for st in starts]
        vs = [v_ref[pl.ds(st, tk), :] for st in starts]
        valids = [(st + col < qe - tq + row) & (st + col < nm + tk)
                  for st, qe, nm in zip(starts, q_ends, nominal)]
        zs = [lax.dot_general(q_heads[s][h], ks[s], NT_DIMS, preferred_element_type=F32) for s, h in pairs]
        log_betas = [jnp.minimum(z, 0.0) - jnp.log(1.0 + jnp.exp(-jnp.abs(z))) for z in zs]
        log_keeps = [jnp.where(valids[s], lb - z, 0.0) for (s, _), lb, z in zip(pairs, log_betas, zs)]
        sums = [jnp.dot(lk.astype(BF16), later, preferred_element_type=F32) for lk in log_keeps]
        weights = [jnp.where(valids[s], jnp.exp(lb + sm[:, :tk]), 0.0).astype(BF16)
                   for (s, _), lb, sm in zip(pairs, log_betas, sums)]
        pvs = [jnp.dot(a, vs[s], preferred_element_type=F32) for (s, _), a in zip(pairs, weights)]
        cmax = [None] * sub
        for (s, h), pv, sm in zip(pairs, pvs, sums):
            if first:
                acc_ref[s, h] = pv
                c = sm[:, tk:]
            else:
                c = c_ref[s, h]
                acc_ref[s, h] += jnp.exp(c) * pv
                c = c + sm[:, tk:]
            c_ref[s, h] = c
            m = jnp.max(c)
            cmax[s] = m if cmax[s] is None else jnp.maximum(cmax[s], m)
        return cmax

    def unfinished(carry):
        b = carry[0]
        need = [(qe - b * tk > 0) & (cm > SB_SKIP_LOG) for qe, cm in zip(q_ends, carry[1:])]
        return functools.reduce(jnp.logical_or, need)

    lax.while_loop(unfinished, lambda carry: (carry[0] + 1, *walk(carry[0], False)),
                   (jnp.int32(1), *walk(0, True)))
    for s in tiles:
        o_ref[s * tq:(s + 1) * tq, :] = jnp.where(head_a, acc_ref[s, 0], acc_ref[s, 1]).astype(o_ref.dtype)


def _stick_breaking(qkv, batch, seq):
    pairs = SB_WIDTH // LANES
    tq, tk, sub = SB_TQ, SB_TK, SB_SUB
    rows = tq * sub
    r = jnp.arange(tk)[:, None]
    c = jnp.arange(tk + LANES)[None, :]
    later = ((c >= tk) | (r > c)).astype(BF16)
    return pl.pallas_call(
        functools.partial(_sb_kernel, tq=tq, tk=tk, sub=sub),
        grid=(batch, pairs, seq // rows),
        in_specs=[pl.BlockSpec((None, rows, LANES), lambda b, p, i: (b, i, p)),
                  pl.BlockSpec((None, seq, LANES), lambda b, p, i: (b, 0, pairs + p)),
                  pl.BlockSpec((None, seq, LANES), lambda b, p, i: (b, 0, 2 * pairs + p)),
                  pl.BlockSpec((tk, tk + LANES), lambda b, p, i: (0, 0))],
        out_specs=pl.BlockSpec((None, rows, LANES), lambda b, p, i: (b, i, p)),
        out_shape=jax.ShapeDtypeStruct((batch, seq, SB_WIDTH), BF16),
        scratch_shapes=[pltpu.VMEM((sub, 2, tq, LANES), F32), pltpu.VMEM((sub, 2, tq, LANES), F32)],
        compiler_params=_params("parallel", "parallel", "parallel"),
        name="stick_breaking",
    )(qkv, qkv, qkv, later)


CAT = DN_HEADS * CHUNK


def _split3(x):
    hi = x.astype(BF16)
    r = x - hi.astype(F32)
    mid = r.astype(BF16)
    return hi, mid, (r - mid.astype(F32)).astype(BF16)


def _block_diag(x, mask):
    return jnp.concatenate([x] * DN_HEADS, axis=0) * mask


def _cat_matmul(l, r, mask):
    return jnp.dot(l.astype(BF16), _block_diag(r.astype(BF16), mask), preferred_element_type=F32)


def _by_head(head, vals):
    return jnp.where(head == 0, vals[0], jnp.where(head == 1, vals[1], jnp.where(head == 2, vals[2], vals[3])))


def _gdn_constants():
    i = jnp.arange(CHUNK)
    tri = (i[:, None] >= i[None, :]).astype(BF16)
    row_head = jnp.arange(CAT)[:, None] // CHUNK
    wide_head = jnp.arange(DN_WIDTH) // DN_HEAD_DIM
    return (jnp.concatenate([tri] * 3, axis=1),
            (row_head == jnp.arange(CAT)[None, :] // CHUNK).astype(BF16),
            (row_head == wide_head[None, :]).astype(BF16),
            (row_head == jnp.concatenate([wide_head, wide_head])[None, :]).astype(BF16))


def _gdn_kernel(qkv_ref, z_ref, s_ref, cw_ref, alog_ref, dt_ref, og_ref, tri3_ref, bd_ref, bdk_ref,
                bdr_ref, o_ref, ubuf, cbuf, state_ref, sbd_ref, *, ts):
    si = pl.program_id(1)
    tail = 8

    @pl.when(si == 0)
    def _():
        state_ref[...] = jnp.zeros_like(state_ref)
        sbd_ref[...] = jnp.zeros_like(sbd_ref)
        ubuf[0:tail, :] = jnp.zeros((tail, 3 * DN_WIDTH), F32)

    @pl.when(si > 0)
    def _():
        ubuf[0:tail, :] = ubuf[ts:ts + tail, :]

    ubuf[tail:tail + ts, :] = qkv_ref[...]
    conv = cw_ref[0:1, :] * ubuf[tail - 3:tail - 3 + ts, :]
    for j in range(1, CONV_K):
        conv = conv + cw_ref[j:j + 1, :] * ubuf[tail - 3 + j:tail - 3 + j + ts, :]
    cbuf[...] = conv * _sigmoid(conv)

    rows = lax.broadcasted_iota(jnp.int32, (CHUNK, CAT), 0)
    lane = lax.broadcasted_iota(jnp.int32, (CHUNK, CAT), 1)
    cols = lane % CHUNK
    head = lane // CHUNK
    incl = rows >= cols
    strict = rows > cols
    upper = rows <= cols
    pair = rows // 2 == cols // 2
    lower_blocks = []
    blk = 2
    while blk < CHUNK:
        lower_blocks.append((rows // (2 * blk) == cols // (2 * blk)) & (rows // blk != cols // blk))
        blk *= 2
    neg_a = -jnp.exp(alog_ref[...])
    dt_bias = dt_ref[...]
    onorm_g = og_ref[...]
    heads = range(DN_HEADS)

    def operands(r0, beta_all, gcol):
        qs, ks, q_decs, k_decs, rhs_v, rhs_k, lasts = [], [], [], [], [], [], []
        for h in heads:
            lo, hi = h * DN_HEAD_DIM, (h + 1) * DN_HEAD_DIM
            q = cbuf[pl.ds(r0, CHUNK), lo:hi]
            k = cbuf[pl.ds(r0, CHUNK), DN_WIDTH + lo:DN_WIDTH + hi]
            v = cbuf[pl.ds(r0, CHUNK), 2 * DN_WIDTH + lo:2 * DN_WIDTH + hi]
            q = q * lax.rsqrt(jnp.sum(q * q, axis=-1, keepdims=True) + NORM_EPS) * (DN_HEAD_DIM ** -0.5)
            k = k * lax.rsqrt(jnp.sum(k * k, axis=-1, keepdims=True) + NORM_EPS)
            beta = beta_all[:, h:h + 1]
            gc = gcol[:, h * CHUNK:h * CHUNK + 1]
            glast = gc[CHUNK - 1:CHUNK, :]
            eg = jnp.exp(gc)
            qs.append(q)
            ks.append(k)
            q_decs.append(q * eg)
            k_decs.append((k * jnp.exp(glast - gc)).astype(BF16))
            rhs_v.append(beta * v)
            rhs_k.append(beta * eg * k)
            lasts.append(jnp.exp(glast))
        return (jnp.concatenate(ks, axis=1).astype(BF16), jnp.concatenate(qs, axis=1).astype(BF16),
                jnp.concatenate(rhs_v + rhs_k, axis=1), jnp.concatenate(q_decs, axis=1).astype(BF16),
                k_decs, lasts)

    def group_body(gi, _):
        group = range(GDN_GROUP)
        r0s = [pl.multiple_of((gi * GDN_GROUP + j) * CHUNK, CHUNK) for j in group]
        sms = [s_ref[pl.ds(r0, CHUNK), :] for r0 in r0s]
        beta_alls = [_sigmoid(sm) for sm in sms]
        g_alls = [neg_a * _softplus(sm + dt_bias) for sm in sms]
        beta_cats = [_by_head(head, [b[:, h:h + 1] for h in heads]) for b in beta_alls]
        g_cats = [_by_head(head, [g[:, DN_HEADS + h:DN_HEADS + h + 1] for h in heads]) for g in g_alls]
        gcols = [jnp.dot(tri3_ref[...], jnp.concatenate(_split3(g), axis=0), preferred_element_type=F32)
                 for g in g_cats]
        grows = [jnp.sum(jnp.where(upper, g, 0.0), axis=0, keepdims=True) for g in g_cats]
        gammas = [jnp.exp(jnp.where(incl, gc - gr, -jnp.inf)) for gc, gr in zip(gcols, grows)]
        ops = [operands(r0, b, gc) for r0, b, gc in zip(r0s, beta_alls, gcols)]
        bdk = bdk_ref[...]
        kqs = [lax.dot_general(jnp.concatenate([kc, qc], axis=0), _block_diag(kc, bdk), NT_DIMS,
                               preferred_element_type=F32) for kc, qc, _, _, _, _ in ops]
        qks = [(kq[CHUNK:] * gm).astype(BF16) for kq, gm in zip(kqs, gammas)]
        a_lows = [jnp.where(strict, b * kq[:CHUNK] * gm, 0.0) for b, kq, gm in zip(beta_cats, kqs, gammas)]
        bd = bd_ref[...]
        ys = [-jnp.where(pair, a, 0.0) for a in a_lows]
        for lower in lower_blocks:
            lows = [jnp.where(lower, a, 0.0) for a in a_lows]
            ts_ = [low + _cat_matmul(low, y, bd) for low, y in zip(lows, ys)]
            ys = [y - t - _cat_matmul(y, t, bd) for y, t in zip(ys, ts_)]
        bdr = bdr_ref[...]
        uws = [op[2] + _cat_matmul(y, op[2], bdr) for y, op in zip(ys, ops)]
        for j in group:
            _, _, _, q_dec, k_decs, lasts = ops[j]
            u, w = uws[j][:, :DN_WIDTH], uws[j][:, DN_WIDTH:]
            ws = jnp.dot(jnp.concatenate([w.astype(BF16), q_dec], axis=0), sbd_ref[...],
                         preferred_element_type=F32)
            vb = (u - ws[:CHUNK]).astype(BF16)
            o = ws[CHUNK:] + jnp.dot(qks[j], _block_diag(vb, bdk), preferred_element_type=F32)
            for h in heads:
                lo, hi = h * DN_HEAD_DIM, (h + 1) * DN_HEAD_DIM
                new = lasts[h] * state_ref[h] + lax.dot_general(k_decs[h], vb[:, lo:hi], TN_DIMS,
                                                                preferred_element_type=F32)
                state_ref[h] = new
                sbd_ref[lo:hi, lo:hi] = new.astype(BF16)
                oh = o[:, lo:hi]
                oh = oh * lax.rsqrt(jnp.mean(oh * oh, axis=-1, keepdims=True) + NORM_EPS) * onorm_g
                zz = z_ref[pl.ds(r0s[j], CHUNK), lo:hi]
                o_ref[pl.ds(r0s[j], CHUNK), lo:hi] = (oh * (zz * _sigmoid(zz))).astype(o_ref.dtype)
        return 0

    lax.fori_loop(0, ts // (CHUNK * GDN_GROUP), group_body, 0)


def _gated_deltanet(pb, ps, conv_w, alog_row, dt_row, onorm_row, batch, seq):
    ts = GDN_TS
    qkv_blocks = 3 * DN_WIDTH // DN_WIDTH
    consts = _gdn_constants()
    const = lambda a: pl.BlockSpec(a.shape, lambda b, s: (0, 0))
    return pl.pallas_call(
        functools.partial(_gdn_kernel, ts=ts),
        grid=(batch, seq // ts),
        in_specs=[pl.BlockSpec((None, ts, 3 * DN_WIDTH), lambda b, s: (b, s, 0)),
                  pl.BlockSpec((None, ts, DN_WIDTH), lambda b, s: (b, s, qkv_blocks)),
                  pl.BlockSpec((None, ts, LANES), lambda b, s: (b, s, 0)),
                  const(conv_w), const(alog_row), const(dt_row), const(onorm_row)]
                 + [const(a) for a in consts],
        out_specs=pl.BlockSpec((None, ts, DN_WIDTH), lambda b, s: (b, s, 0)),
        out_shape=jax.ShapeDtypeStruct((batch, seq, DN_WIDTH), BF16),
        scratch_shapes=[pltpu.VMEM((ts + 8, 3 * DN_WIDTH), F32),
                        pltpu.VMEM((ts, 3 * DN_WIDTH), F32),
                        pltpu.VMEM((DN_HEADS, DN_HEAD_DIM, DN_HEAD_DIM), F32),
                        pltpu.VMEM((DN_WIDTH, DN_WIDTH), BF16)],
        compiler_params=_params("parallel", "arbitrary"),
        name="gated_deltanet",
    )(pb, pb, ps, conv_w, alog_row, dt_row, onorm_row, *consts)


def _top2_of4(vals):
    m1 = jnp.maximum(jnp.maximum(vals[0], vals[1]), jnp.maximum(vals[2], vals[3]))
    i1 = jnp.where(vals[0] == m1, 0, jnp.where(vals[1] == m1, 1, jnp.where(vals[2] == m1, 2, 3)))
    rest = [jnp.where(i1 == r, -jnp.inf, vals[r]) for r in range(4)]
    m2 = jnp.maximum(jnp.maximum(rest[0], rest[1]), jnp.maximum(rest[2], rest[3]))
    i2 = jnp.where(rest[0] == m2, 0, jnp.where(rest[1] == m2, 1, jnp.where(rest[2] == m2, 2, 3)))
    return m1, i1, m2, i2


def _pick4(idx, vals):
    return jnp.where(idx == 0, vals[0], jnp.where(idx == 1, vals[1], jnp.where(idx == 2, vals[2], vals[3])))


def _merge_kernel(oa_ref, ob_ref, ga_ref, gb_ref, x_ref, wa_ref, wb_ref, wo_ref, g_ref, b_ref,
                  wr_ref, rb_ref, h_ref, ri_ref, cnt_ref, carry_ref, *, tm, alpha):
    t = pl.program_id(0)

    @pl.when(t == 0)
    def _():
        carry_ref[...] = jnp.zeros_like(carry_ref)

    ya = jnp.dot(oa_ref[...], wa_ref[...], preferred_element_type=F32)
    yb = jnp.dot(ob_ref[...], wb_ref[...], preferred_element_type=F32)
    merged = _sigmoid(ga_ref[...]) * ya + _sigmoid(gb_ref[...]) * yb
    mix = jnp.dot(merged.astype(BF16), wo_ref[...], preferred_element_type=F32)
    h = _layer_norm(alpha * x_ref[...] + mix, g_ref[...], b_ref[...])
    h_ref[:, :D_MODEL] = h

    logits = jnp.dot(h.astype(BF16), wr_ref[...], preferred_element_type=F32)
    lt = logits.T[:N_EXPERTS, :]
    ex = jnp.exp(lt - jnp.max(lt, axis=0, keepdims=True))
    probs = ex / jnp.sum(ex, axis=0, keepdims=True)
    sel = probs + rb_ref[...]
    sel_rows = [sel[e:e + 1, :] for e in range(N_EXPERTS)]
    prob_rows = [probs[e:e + 1, :] for e in range(N_EXPERTS)]
    scores = []
    for g in range(N_GROUPS):
        m1, _, m2, _ = _top2_of4(sel_rows[g * EXPERTS_PER_GROUP:(g + 1) * EXPERTS_PER_GROUP])
        scores.append(m1 + m2)
    best = jnp.maximum(jnp.maximum(scores[0], scores[1]), jnp.maximum(scores[2], scores[3]))
    grp = jnp.where(scores[0] == best, 0, jnp.where(scores[1] == best, 1, jnp.where(scores[2] == best, 2, 3)))
    in_grp = [_pick4(grp, [sel_rows[g * EXPERTS_PER_GROUP + r] for g in range(N_GROUPS)])
              for r in range(EXPERTS_PER_GROUP)]
    p_grp = [_pick4(grp, [prob_rows[g * EXPERTS_PER_GROUP + r] for g in range(N_GROUPS)])
             for r in range(EXPERTS_PER_GROUP)]
    _, l0, _, l1 = _top2_of4(in_grp)
    p0 = _pick4(l0, p_grp)
    p1 = _pick4(l1, p_grp)
    w0 = p0 / (p0 + p1)
    w1 = p1 / (p0 + p1)
    r128 = lax.broadcasted_iota(jnp.int32, (LANES, tm), 0)
    local_w = jnp.zeros((LANES, tm), F32)
    for r in range(EXPERTS_PER_GROUP):
        local_w = jnp.where(r128 == r, jnp.where(l0 == r, w0, jnp.where(l1 == r, w1, 0.0)), local_w)
    h_ref[:, D_MODEL:] = local_w.T

    gidx = lax.broadcasted_iota(jnp.int32, (8, tm), 0)
    hit = gidx == grp
    onehot = jnp.where(hit, 1.0, 0.0)
    trow = lax.broadcasted_iota(jnp.int32, (tm, tm), 0)
    tcol = lax.broadcasted_iota(jnp.int32, (tm, tm), 1)
    earlier = jnp.where(trow < tcol, 1.0, 0.0).astype(BF16)
    before = jnp.dot(onehot.astype(BF16), earlier, preferred_element_type=F32) + carry_ref[...]
    rank = jnp.sum(jnp.where(hit, before, 0.0), axis=0, keepdims=True).astype(jnp.int32)
    carry = carry_ref[...] + jnp.sum(onehot, axis=1, keepdims=True)
    carry_ref[...] = carry
    cnt_ref[...] = jnp.broadcast_to(carry, cnt_ref.shape).astype(jnp.int32)
    ri_ref[...] = jnp.where(gidx == 0, grp, jnp.where(gidx == 1, rank, 0))


def _merge(o_a, o_b, gates, x, w_br_a, w_br_b, w_o, ln_g, ln_b, w_router, router_bias, alpha):
    n = x.shape[0]
    tm = MERGE_TM
    full = lambda shape: pl.BlockSpec(shape, lambda t: (0,) * len(shape))
    return pl.pallas_call(
        functools.partial(_merge_kernel, tm=tm, alpha=alpha),
        grid=(n // tm,),
        in_specs=[pl.BlockSpec((tm, SB_WIDTH), lambda t: (t, 0)),
                  pl.BlockSpec((tm, DN_WIDTH), lambda t: (t, 0)),
                  pl.BlockSpec((tm, D_MODEL), lambda t: (t, 0)),
                  pl.BlockSpec((tm, D_MODEL), lambda t: (t, 1)),
                  pl.BlockSpec((tm, D_MODEL), lambda t: (t, 0)),
                  full((SB_WIDTH, D_MODEL)), full((DN_WIDTH, D_MODEL)), full((D_MODEL, D_MODEL)),
                  full((1, D_MODEL)), full((1, D_MODEL)),
                  full((D_MODEL, LANES)), full((N_EXPERTS, 1))],
        out_specs=[pl.BlockSpec((tm, H_EXT), lambda t: (t, 0)),
                   pl.BlockSpec((8, tm), lambda t: (0, t)),
                   pl.BlockSpec((8, LANES), lambda t: (0, 0))],
        out_shape=[jax.ShapeDtypeStruct((n, H_EXT), F32),
                   jax.ShapeDtypeStruct((8, n), jnp.int32),
                   jax.ShapeDtypeStruct((8, LANES), jnp.int32)],
        scratch_shapes=[pltpu.VMEM((8, 1), F32)],
        compiler_params=_params("arbitrary"),
        name="merge_ln_route",
    )(o_a, o_b, gates, gates, x, w_br_a, w_br_b, w_o, ln_g, ln_b, w_router, router_bias)


def _group_ffn_kernel(blk_grp_ref, src_ref, dst_ref, h_hbm, wg_ref, wu_ref, wd_ref, f_hbm,
                      xbuf, obuf, gsem, ssem, *, rb, n_blk):
    del blk_grp_ref
    i = pl.program_id(0)
    slot = i % 2
    other = 1 - slot

    def gather_row(block, r, s):
        return pltpu.make_async_copy(h_hbm.at[pl.ds(src_ref[block * rb + r], 1), :],
                                     xbuf.at[s, pl.ds(r, 1), :], gsem.at[s])

    def scatter_row(block, r, s):
        return pltpu.make_async_copy(obuf.at[s, pl.ds(r, 1), :],
                                     f_hbm.at[pl.ds(dst_ref[(block + 2) * rb + r], 1), :], ssem.at[s])

    @pl.when(i == 0)
    def _():
        obuf[...] = jnp.zeros_like(obuf)

        def issue(r, _):
            gather_row(0, r, 0).start()
            scatter_row(-2, r, 1).start()
            return 0

        lax.fori_loop(0, rb, issue, 0)
        pltpu.make_async_copy(obuf.at[1], f_hbm.at[pl.ds(0, rb), :], ssem.at[1]).wait()

    pltpu.make_async_copy(h_hbm.at[pl.ds(0, rb), :], xbuf.at[slot], gsem.at[slot]).wait()

    @pl.when(i > 0)
    def _():
        pltpu.make_async_copy(obuf.at[slot], f_hbm.at[pl.ds(0, rb), :], ssem.at[slot]).wait()

    @pl.when(i < n_blk)
    def _():
        x = xbuf[slot, :, :D_MODEL].astype(BF16)
        part = rb // EXPERTS_PER_GROUP
        for e in range(EXPERTS_PER_GROUP):
            for r in range(e * part, (e + 1) * part):
                gather_row(i + 1, r, other).start()
                scatter_row(i - 1, r, other).start()
            gate = jnp.dot(x, wg_ref[e], preferred_element_type=F32)
            up = jnp.dot(x, wu_ref[e], preferred_element_type=F32)
            hid = (gate * _sigmoid(gate) * up).astype(BF16)
            y = xbuf[slot, :, D_MODEL + e:D_MODEL + e + 1] * jnp.dot(hid, wd_ref[e], preferred_element_type=F32)
            if e == 0:
                obuf[slot] = y
            else:
                obuf[slot] += y

    @pl.when(i == n_blk)
    def _():
        def issue(r, _):
            scatter_row(n_blk - 1, r, other).start()
            return 0

        lax.fori_loop(0, rb, issue, 0)
        pltpu.make_async_copy(obuf.at[other], f_hbm.at[pl.ds(0, rb), :], ssem.at[other]).wait()


def _group_ffn(h_ext, blk_grp, row_src, row_dst, w_gate, w_up, w_down, n_tok):
    rb = ROUTE_BLOCK
    n_blk = row_src.shape[0] // rb - 1
    wspec = lambda a, b: pl.BlockSpec((EXPERTS_PER_GROUP, a, b), lambda i, bg, rs, rd: (bg[i], 0, 0))
    return pl.pallas_call(
        functools.partial(_group_ffn_kernel, rb=rb, n_blk=n_blk),
        grid_spec=pltpu.PrefetchScalarGridSpec(
            num_scalar_prefetch=3,
            grid=(n_blk + 1,),
            in_specs=[pl.BlockSpec(memory_space=pl.ANY),
                      wspec(D_MODEL, D_EXPERT), wspec(D_MODEL, D_EXPERT), wspec(D_EXPERT, D_MODEL)],
            out_specs=pl.BlockSpec(memory_space=pl.ANY),
            scratch_shapes=[pltpu.VMEM((2, rb, H_EXT), F32), pltpu.VMEM((2, rb, D_MODEL), F32),
                            pltpu.SemaphoreType.DMA((2,)), pltpu.SemaphoreType.DMA((2,))]),
        out_shape=jax.ShapeDtypeStruct((n_tok + 2 * rb, D_MODEL), F32),
        compiler_params=_params("arbitrary"),
        name="group_ffn",
    )(blk_grp, row_src, row_dst, h_ext, w_gate, w_up, w_down)


def _ffn_norm_kernel(h_ref, f_ref, g_ref, b_ref, o_ref, ob_ref, *, alpha):
    out = _layer_norm(alpha * h_ref[...] + f_ref[...], g_ref[...], b_ref[...])
    o_ref[...] = out
    ob_ref[...] = out.astype(BF16)


def _ffn_norm(h_ext, ffn, ln_g, ln_b, alpha):
    n = h_ext.shape[0]
    tm = NORM_TM
    row = pl.BlockSpec((tm, D_MODEL), lambda t: (t, 0))
    vec = pl.BlockSpec((1, D_MODEL), lambda t: (0, 0))
    return pl.pallas_call(
        functools.partial(_ffn_norm_kernel, alpha=alpha),
        grid=(n // tm,),
        in_specs=[row, row, vec, vec],
        out_specs=[row, row],
        out_shape=[jax.ShapeDtypeStruct((n, D_MODEL), F32), jax.ShapeDtypeStruct((n, D_MODEL), BF16)],
        compiler_params=_params("parallel"),
        name="ffn_norm",
    )(h_ext, ffn, ln_g, ln_b)


def _route_rows(ri, cnt, n_tok):
    rb = ROUTE_BLOCK
    counts = cnt[:N_GROUPS, 0]
    padded = (counts + rb - 1) // rb * rb
    pends = jnp.cumsum(padded)
    dest = (pends - padded)[ri[0]] + ri[1]
    p_rows = n_tok + N_GROUPS * rb
    row_tok = jnp.full((p_rows,), -1, jnp.int32).at[dest].set(jnp.arange(n_tok, dtype=jnp.int32))
    spare = n_tok + jnp.arange(-2 * rb, p_rows, dtype=jnp.int32) % (2 * rb)
    row_src = jnp.concatenate([jnp.maximum(row_tok, 0), jnp.zeros((rb,), jnp.int32)])
    row_dst = jnp.concatenate([jnp.full((2 * rb,), -1, jnp.int32), row_tok])
    row_dst = jnp.where(row_dst < 0, spare, row_dst)
    blk_start = jnp.arange(p_rows // rb + 1, dtype=jnp.int32) * rb
    blk_grp = jnp.minimum(jnp.searchsorted(pends, blk_start, side="right"), N_GROUPS - 1).astype(jnp.int32)
    return row_src, row_dst, blk_grp


def _lane_row(values, offset):
    return jnp.zeros((1, LANES), F32).at[0, offset:offset + values.shape[0]].set(values.astype(F32))


def kernel(x, w_in, conv_w, a_log, dt_bias, onorm_g, w_br_a, w_br_b, w_o, ln1_g, ln1_b, w_router,
           router_bias, w_gate, w_up, w_down, ln2_g, ln2_b):
    batch, seq, d = x.shape
    depth = w_in.shape[0]
    n_tok = batch * seq
    alpha = (2 * depth) ** 0.25
    a_end = 3 * SB_WIDTH
    b_end = a_end + 4 * DN_WIDTH
    s_end = b_end + 2 * DN_HEADS
    w_a = w_in[:, :, :a_end].astype(BF16)
    w_b = w_in[:, :, a_end:b_end].astype(BF16)
    w_s = jnp.pad(w_in[:, :, b_end:s_end], ((0, 0), (0, 0), (0, LANES - 2 * DN_HEADS))).astype(BF16)
    w_g = w_in[:, :, s_end:].astype(BF16)
    w_r = jnp.pad(w_router, ((0, 0), (0, LANES - N_EXPERTS))).astype(BF16)
    r_bias = router_bias.astype(F32).reshape(N_EXPERTS, 1)
    w_br_a, w_br_b, w_o = w_br_a.astype(BF16), w_br_b.astype(BF16), w_o.astype(BF16)
    w_gate, w_up, w_down = w_gate.astype(BF16), w_up.astype(BF16), w_down.astype(BF16)

    xf = x.reshape(n_tok, d)
    xb = xf.astype(BF16)
    for l in range(depth):
        pa = _matmul(xb, w_a[l], BF16, PROJ_TN).reshape(batch, seq, a_end)
        pb = _matmul(xb, w_b[l], F32, PROJ_TN).reshape(batch, seq, 4 * DN_WIDTH)
        ps = _matmul(xb, w_s[l], F32, LANES).reshape(batch, seq, LANES)
        gates = _matmul(xb, w_g[l], F32, PROJ_TN)
        o_a = _stick_breaking(pa, batch, seq).reshape(n_tok, SB_WIDTH)
        o_b = _gated_deltanet(pb, ps, conv_w[l].astype(F32), _lane_row(a_log[l], DN_HEADS),
                              _lane_row(dt_bias[l], DN_HEADS), onorm_g[l].astype(F32).reshape(1, DN_HEAD_DIM),
                              batch, seq).reshape(n_tok, DN_WIDTH)
        h_ext, ri, cnt = _merge(o_a, o_b, gates, xf, w_br_a[l], w_br_b[l], w_o[l],
                                ln1_g[l].reshape(1, d), ln1_b[l].reshape(1, d), w_r, r_bias, alpha)
        row_src, row_dst, blk_grp = _route_rows(ri, cnt, n_tok)
        ffn = _group_ffn(h_ext, blk_grp, row_src, row_dst, w_gate[l], w_up[l], w_down[l], n_tok)
        xf, xb = _ffn_norm(h_ext, ffn, ln2_g[l].reshape(1, d), ln2_b[l].reshape(1, d), alpha)
    return xf.reshape(batch, seq, d)
```

```python
import functools

import jax
import jax.numpy as jnp
from jax import lax
from jax.experimental import pallas as pl
from jax.experimental.pallas import tpu as pltpu

D_MODEL = 1024
SB_HEADS = 8
SB_HEAD_DIM = 64
SB_WIDTH = SB_HEADS * SB_HEAD_DIM
DN_HEADS = 4
DN_HEAD_DIM = 128
DN_WIDTH = DN_HEADS * DN_HEAD_DIM
CONV_K = 4
CHUNK = 64
N_EXPERTS = 16
N_GROUPS = 4
EXPERTS_PER_GROUP = N_EXPERTS // N_GROUPS
D_EXPERT = 512
LN_EPS = 1e-5
NORM_EPS = 1e-6

LANES = 128
H_EXT = D_MODEL + LANES
VMEM_LIMIT = 48 * 1024 * 1024

PROJ_TM = 512
SB_TQ = 128
SB_TK = 256
SB_SUB = 4
GDN_TS = 512
GDN_GROUP = 8
MERGE_TM = 256
ROUTE_BLOCK = 256
NORM_TM = 512
SB_SKIP_LOG = -60.0

BF16 = jnp.bfloat16
F32 = jnp.float32
NT_DIMS = (((1,), (1,)), ((), ()))
TN_DIMS = (((0,), (0,)), ((), ()))


def _params(*semantics):
    return pltpu.CompilerParams(dimension_semantics=semantics, vmem_limit_bytes=VMEM_LIMIT)


def _sigmoid(x):
    return 1.0 / (1.0 + jnp.exp(-x))


def _softplus(x):
    return jnp.maximum(x, 0.0) + jnp.log(1.0 + jnp.exp(-jnp.abs(x)))


def _layer_norm(y, g, b):
    mu = jnp.mean(y, axis=-1, keepdims=True)
    yc = y - mu
    var = jnp.mean(yc * yc, axis=-1, keepdims=True)
    return yc * lax.rsqrt(var + LN_EPS) * g + b


def _proj_kernel(x_ref, wa_ref, wb_ref, ws_ref, wg_ref, cw_ref, oa_ref, oc_ref, oz_ref, os_ref, og_ref,
                 ubuf, *, tm, tiles_per_seq):
    i = pl.program_id(0)
    tail = 8
    qkv = 3 * DN_WIDTH

    @pl.when(i == 0)
    def _():
        ubuf[tm:tm + tail, :] = jnp.zeros((tail, qkv), F32)

    x = x_ref[...]
    oa_ref[...] = jnp.dot(x, wa_ref[...], preferred_element_type=F32).astype(oa_ref.dtype)
    pb = jnp.dot(x, wb_ref[...], preferred_element_type=F32)
    oz_ref[...] = pb[:, qkv:].astype(oz_ref.dtype)
    prev = ubuf[tm:tm + tail, :]
    ubuf[0:tail, :] = jnp.where(i % tiles_per_seq == 0, jnp.zeros_like(prev), prev)
    ubuf[tail:tail + tm, :] = pb[:, :qkv]
    os_ref[...] = jnp.dot(x, ws_ref[...], preferred_element_type=F32)
    gate_cols = wg_ref.shape[1]
    pieces = 4
    step = gate_cols // pieces
    lane_tiles = qkv // LANES
    for p in range(pieces):
        for c in range(p * lane_tiles // pieces, (p + 1) * lane_tiles // pieces):
            lo, hi = c * LANES, (c + 1) * LANES
            u = ubuf[:, lo:hi]
            conv = cw_ref[CONV_K - 1:CONV_K, lo:hi] * u[tail:, :]
            for j in range(CONV_K - 1):
                conv = conv + cw_ref[j:j + 1, lo:hi] * pltpu.roll(u, CONV_K - 1 - j, axis=0)[tail:, :]
            oc_ref[:, lo:hi] = (conv * _sigmoid(conv)).astype(oc_ref.dtype)
        og_ref[:, p * step:(p + 1) * step] = jnp.dot(
            x, wg_ref[:, p * step:(p + 1) * step], preferred_element_type=F32).astype(og_ref.dtype)


def _project(x, w_a, w_b, w_s, w_g, conv_w, seq):
    m, k = x.shape
    tm = PROJ_TM
    qkv = 3 * DN_WIDTH
    widths = (w_a.shape[1], qkv, w_b.shape[1] - qkv, w_s.shape[1], w_g.shape[1])
    dtypes = (BF16, BF16, BF16, F32, BF16)
    const = lambda a: pl.BlockSpec(a.shape, lambda i: (0, 0))
    return pl.pallas_call(
        functools.partial(_proj_kernel, tm=tm, tiles_per_seq=seq // tm),
        grid=(m // tm,),
        in_specs=[pl.BlockSpec((tm, k), lambda i: (i, 0)), const(w_a), const(w_b), const(w_s), const(w_g),
                  const(conv_w)],
        out_specs=[pl.BlockSpec((tm, n), lambda i: (i, 0)) for n in widths],
        out_shape=[jax.ShapeDtypeStruct((m, n), dt) for n, dt in zip(widths, dtypes)],
        scratch_shapes=[pltpu.VMEM((tm + 8, qkv), F32)],
        compiler_params=_params("arbitrary"),
        name="proj",
    )(x, w_a, w_b, w_s, w_g, conv_w)


def _sb_kernel(q_ref, k_ref, v_ref, later_ref, o_ref, acc_ref, c_ref, *, tq, tk, sub):
    i = pl.program_id(2)
    lane = lax.broadcasted_iota(jnp.int32, (1, LANES), 1)
    head_a = lane < SB_HEAD_DIM
    row = lax.broadcasted_iota(jnp.int32, (tq, tk), 0)
    col = lax.broadcasted_iota(jnp.int32, (tq, tk), 1)
    later = later_ref[...]
    tiles = range(sub)
    pairs = [(s, h) for s in tiles for h in range(2)]
    q_ends = [(i * sub + s + 1) * tq for s in tiles]
    q_heads = []
    for s in tiles:
        q = q_ref[s * tq:(s + 1) * tq, :] * jnp.asarray(SB_HEAD_DIM ** -0.5, q_ref.dtype)
        zero = jnp.zeros_like(q)
        q_heads.append((jnp.where(head_a, q, zero), jnp.where(head_a, zero, q)))

    def walk(b, first):
        nominal = [qe - (b + 1) * tk for qe in q_ends]
        starts = [pl.multiple_of(jnp.maximum(nm, 0), tq) for nm in nominal]
        ks = [k_ref[pl.ds(st, tk), :] for st in starts]
        vs = [v_ref[pl.ds(st, tk), :] for st in starts]
        valids = [(st + col < qe - tq + row) & (st + col < nm + tk)
                  for st, qe, nm in zip(starts, q_ends, nominal)]
        zs = [lax.dot_general(q_heads[s][h], ks[s], NT_DIMS, preferred_element_type=F32) for s, h in pairs]
        log_betas = [jnp.minimum(z, 0.0) - jnp.log(1.0 + jnp.exp(-jnp.abs(z))) for z in zs]
        log_keeps = [jnp.where(valids[s], lb - z, 0.0) for (s, _), lb, z in zip(pairs, log_betas, zs)]
        sums = [jnp.dot(lk.astype(BF16), later, preferred_element_type=F32) for lk in log_keeps]
        weights = [jnp.where(valids[s], jnp.exp(lb + sm[:, :tk]), 0.0).astype(BF16)
                   for (s, _), lb, sm in zip(pairs, log_betas, sums)]
        pvs = [jnp.dot(a, vs[s], preferred_element_type=F32) for (s, _), a in zip(pairs, weights)]
        cmax = [None] * sub
        for (s, h), pv, sm in zip(pairs, pvs, sums):
            if first:
                acc_ref[s, h] = pv
                c = sm[:, tk:]
            else:
                c = c_ref[s, h]
                acc_ref[s, h] += jnp.exp(c) * pv
                c = c + sm[:, tk:]
            c_ref[s, h] = c
            m = jnp.max(c)
            cmax[s] = m if cmax[s] is None else jnp.maximum(cmax[s], m)
        return cmax

    def unfinished(carry):
        b = carry[0]
        need = [(qe - b * tk > 0) & (cm > SB_SKIP_LOG) for qe, cm in zip(q_ends, carry[1:])]
        return functools.reduce(jnp.logical_or, need)

    lax.while_loop(unfinished, lambda carry: (carry[0] + 1, *walk(carry[0], False)),
                   (jnp.int32(1), *walk(0, True)))
    for s in tiles:
        o_ref[s * tq:(s + 1) * tq, :] = jnp.where(head_a, acc_ref[s, 0], acc_ref[s, 1]).astype(o_ref.dtype)


def _stick_breaking(qkv, batch, seq):
    pairs = SB_WIDTH // LANES
    tq, tk, sub = SB_TQ, SB_TK, SB_SUB
    rows = tq * sub
    r = jnp.arange(tk)[:, None]
    c = jnp.arange(tk + LANES)[None, :]
    later = ((c >= tk) | (r > c)).astype(BF16)
    return pl.pallas_call(
        functools.partial(_sb_kernel, tq=tq, tk=tk, sub=sub),
        grid=(batch, pairs, seq // rows),
        in_specs=[pl.BlockSpec((None, rows, LANES), lambda b, p, i: (b, i, p)),
                  pl.BlockSpec((None, seq, LANES), lambda b, p, i: (b, 0, pairs + p)),
                  pl.BlockSpec((None, seq, LANES), lambda b, p, i: (b, 0, 2 * pairs + p)),
                  pl.BlockSpec((tk, tk + LANES), lambda b, p, i: (0, 0))],
        out_specs=pl.BlockSpec((None, rows, LANES), lambda b, p, i: (b, i, p)),
        out_shape=jax.ShapeDtypeStruct((batch, seq, SB_WIDTH), BF16),
        scratch_shapes=[pltpu.VMEM((sub, 2, tq, LANES), F32), pltpu.VMEM((sub, 2, tq, LANES), F32)],
        compiler_params=_params("parallel", "parallel", "parallel"),
        name="stick_breaking",
    )(qkv, qkv, qkv, later)


CAT = DN_HEADS * CHUNK


def _split3(x):
    hi = x.astype(BF16)
    r = x - hi.astype(F32)
    mid = r.astype(BF16)
    return hi, mid, (r - mid.astype(F32)).astype(BF16)


def _block_diag(x, mask):
    return jnp.concatenate([x] * DN_HEADS, axis=0) * mask


def _cat_matmul(l, r, mask):
    return jnp.dot(l.astype(BF16), _block_diag(r.astype(BF16), mask), preferred_element_type=F32)


def _by_head(head, vals):
    return jnp.where(head == 0, vals[0], jnp.where(head == 1, vals[1], jnp.where(head == 2, vals[2], vals[3])))


def _gdn_constants():
    i = jnp.arange(CHUNK)
    tri = (i[:, None] >= i[None, :]).astype(BF16)
    row_head = jnp.arange(CAT)[:, None] // CHUNK
    wide_head = jnp.arange(DN_WIDTH) // DN_HEAD_DIM
    return (jnp.concatenate([tri] * 3, axis=1),
            (row_head == jnp.arange(CAT)[None, :] // CHUNK).astype(BF16),
            (row_head == wide_head[None, :]).astype(BF16),
            (row_head == jnp.concatenate([wide_head, wide_head])[None, :]).astype(BF16))


def _gdn_kernel(qkv_ref, z_ref, s_ref, alog_ref, dt_ref, og_ref, tri3_ref, bd_ref, bdk_ref,
                bdr_ref, o_ref, state_ref, sbd_ref, *, ts):
    @pl.when(pl.program_id(1) == 0)
    def _():
        state_ref[...] = jnp.zeros_like(state_ref)
        sbd_ref[...] = jnp.zeros_like(sbd_ref)

    rows = lax.broadcasted_iota(jnp.int32, (CHUNK, CAT), 0)
    lane = lax.broadcasted_iota(jnp.int32, (CHUNK, CAT), 1)
    cols = lane % CHUNK
    head = lane // CHUNK
    incl = rows >= cols
    strict = rows > cols
    upper = rows <= cols
    pair = rows // 2 == cols // 2
    lower_blocks = []
    blk = 2
    while blk < CHUNK:
        lower_blocks.append((rows // (2 * blk) == cols // (2 * blk)) & (rows // blk != cols // blk))
        blk *= 2
    neg_a = -jnp.exp(alog_ref[...])
    dt_bias = dt_ref[...]
    onorm_g = og_ref[...]
    heads = range(DN_HEADS)

    def operands(r0, beta_all, gcol):
        qs, ks, q_decs, k_decs, rhs_v, rhs_k, lasts = [], [], [], [], [], [], []
        for h in heads:
            lo, hi = h * DN_HEAD_DIM, (h + 1) * DN_HEAD_DIM
            q = qkv_ref[pl.ds(r0, CHUNK), lo:hi].astype(F32)
            k = qkv_ref[pl.ds(r0, CHUNK), DN_WIDTH + lo:DN_WIDTH + hi].astype(F32)
            v = qkv_ref[pl.ds(r0, CHUNK), 2 * DN_WIDTH + lo:2 * DN_WIDTH + hi].astype(F32)
            q = q * lax.rsqrt(jnp.sum(q * q, axis=-1, keepdims=True) + NORM_EPS) * (DN_HEAD_DIM ** -0.5)
            k = k * lax.rsqrt(jnp.sum(k * k, axis=-1, keepdims=True) + NORM_EPS)
            beta = beta_all[:, h:h + 1]
            gc = gcol[:, h * CHUNK:h * CHUNK + 1]
            glast = gc[CHUNK - 1:CHUNK, :]
            eg = jnp.exp(gc)
            qs.append(q)
            ks.append(k)
            q_decs.append(q * eg)
            k_decs.append((k * jnp.exp(glast - gc)).astype(BF16))
            rhs_v.append(beta * v)
            rhs_k.append(beta * eg * k)
            lasts.append(jnp.exp(glast))
        return (jnp.concatenate(ks, axis=1).astype(BF16), jnp.concatenate(qs, axis=1).astype(BF16),
                jnp.concatenate(rhs_v + rhs_k, axis=1), jnp.concatenate(q_decs, axis=1).astype(BF16),
                k_decs, lasts)

    def group_body(gi, _):
        group = range(GDN_GROUP)
        r0s = [pl.multiple_of((gi * GDN_GROUP + j) * CHUNK, CHUNK) for j in group]
        sms = [s_ref[pl.ds(r0, CHUNK), :] for r0 in r0s]
        beta_alls = [_sigmoid(sm) for sm in sms]
        g_alls = [neg_a * _softplus(sm + dt_bias) for sm in sms]
        beta_cats = [_by_head(head, [b[:, h:h + 1] for h in heads]) for b in beta_alls]
        g_cats = [_by_head(head, [g[:, DN_HEADS + h:DN_HEADS + h + 1] for h in heads]) for g in g_alls]
        gcols = [jnp.dot(tri3_ref[...], jnp.concatenate(_split3(g), axis=0), preferred_element_type=F32)
                 for g in g_cats]
        grows = [jnp.sum(jnp.where(upper, g, 0.0), axis=0, keepdims=True) for g in g_cats]
        gammas = [jnp.exp(jnp.where(incl, gc - gr, -jnp.inf)) for gc, gr in zip(gcols, grows)]
        ops = [operands(r0, b, gc) for r0, b, gc in zip(r0s, beta_alls, gcols)]
        bdk = bdk_ref[...]
        kqs = [lax.dot_general(jnp.concatenate([kc, qc], axis=0), _block_diag(kc, bdk), NT_DIMS,
                               preferred_element_type=F32) for kc, qc, _, _, _, _ in ops]
        qks = [(kq[CHUNK:] * gm).astype(BF16) for kq, gm in zip(kqs, gammas)]
        a_lows = [jnp.where(strict, b * kq[:CHUNK] * gm, 0.0) for b, kq, gm in zip(beta_cats, kqs, gammas)]
        bd = bd_ref[...]
        ys = [-jnp.where(pair, a, 0.0) for a in a_lows]
        for lower in lower_blocks:
            lows = [jnp.where(lower, a, 0.0) for a in a_lows]
            ts_ = [low + _cat_matmul(low, y, bd) for low, y in zip(lows, ys)]
            ys = [y - t - _cat_matmul(y, t, bd) for y, t in zip(ys, ts_)]
        bdr = bdr_ref[...]
        uws = [op[2] + _cat_matmul(y, op[2], bdr) for y, op in zip(ys, ops)]
        for j in group:
            _, _, _, q_dec, k_decs, lasts = ops[j]
            u, w = uws[j][:, :DN_WIDTH], uws[j][:, DN_WIDTH:]
            ws = jnp.dot(jnp.concatenate([w.astype(BF16), q_dec], axis=0), sbd_ref[...],
                         preferred_element_type=F32)
            vb = (u - ws[:CHUNK]).astype(BF16)
            o = ws[CHUNK:] + jnp.dot(qks[j], _block_diag(vb, bdk), preferred_element_type=F32)
            for h in heads:
                lo, hi = h * DN_HEAD_DIM, (h + 1) * DN_HEAD_DIM
                new = lasts[h] * state_ref[h] + lax.dot_general(k_decs[h], vb[:, lo:hi], TN_DIMS,
                                                                preferred_element_type=F32)
                state_ref[h] = new
                sbd_ref[lo:hi, lo:hi] = new.astype(BF16)
                oh = o[:, lo:hi]
                oh = oh * lax.rsqrt(jnp.mean(oh * oh, axis=-1, keepdims=True) + NORM_EPS) * onorm_g
                zz = z_ref[pl.ds(r0s[j], CHUNK), lo:hi].astype(F32)
                o_ref[pl.ds(r0s[j], CHUNK), lo:hi] = (oh * (zz * _sigmoid(zz))).astype(o_ref.dtype)
        return 0

    lax.fori_loop(0, ts // (CHUNK * GDN_GROUP), group_body, 0)


def _gated_deltanet(pc, pz, ps, alog_row, dt_row, onorm_row, batch, seq):
    ts = GDN_TS
    consts = _gdn_constants()
    const = lambda a: pl.BlockSpec(a.shape, lambda b, s: (0, 0))
    tile = lambda width: pl.BlockSpec((None, ts, width), lambda b, s: (b, s, 0))
    return pl.pallas_call(
        functools.partial(_gdn_kernel, ts=ts),
        grid=(batch, seq // ts),
        in_specs=[tile(3 * DN_WIDTH), tile(DN_WIDTH), tile(LANES), const(alog_row), const(dt_row),
                  const(onorm_row)] + [const(a) for a in consts],
        out_specs=tile(DN_WIDTH),
        out_shape=jax.ShapeDtypeStruct((batch, seq, DN_WIDTH), BF16),
        scratch_shapes=[pltpu.VMEM((DN_HEADS, DN_HEAD_DIM, DN_HEAD_DIM), F32),
                        pltpu.VMEM((DN_WIDTH, DN_WIDTH), BF16)],
        compiler_params=_params("parallel", "arbitrary"),
        name="gated_deltanet",
    )(pc, pz, ps, alog_row, dt_row, onorm_row, *consts)


def _top2_of4(vals):
    m1 = jnp.maximum(jnp.maximum(vals[0], vals[1]), jnp.maximum(vals[2], vals[3]))
    i1 = jnp.where(vals[0] == m1, 0, jnp.where(vals[1] == m1, 1, jnp.where(vals[2] == m1, 2, 3)))
    rest = [jnp.where(i1 == r, -jnp.inf, vals[r]) for r in range(4)]
    m2 = jnp.maximum(jnp.maximum(rest[0], rest[1]), jnp.maximum(rest[2], rest[3]))
    i2 = jnp.where(rest[0] == m2, 0, jnp.where(rest[1] == m2, 1, jnp.where(rest[2] == m2, 2, 3)))
    return m1, i1, m2, i2


def _pick4(idx, vals):
    return jnp.where(idx == 0, vals[0], jnp.where(idx == 1, vals[1], jnp.where(idx == 2, vals[2], vals[3])))


def _merge_kernel(oa_ref, ob_ref, ga_ref, gb_ref, x_ref, wa_ref, wb_ref, wo_ref, g_ref, b_ref,
                  wr_ref, rb_ref, h_ref, ri_ref, cnt_ref, carry_ref, *, tm, alpha):
    t = pl.program_id(0)

    @pl.when(t == 0)
    def _():
        carry_ref[...] = jnp.zeros_like(carry_ref)

    ya = jnp.dot(oa_ref[...], wa_ref[...], preferred_element_type=F32)
    yb = jnp.dot(ob_ref[...], wb_ref[...], preferred_element_type=F32)
    merged = _sigmoid(ga_ref[...].astype(F32)) * ya + _sigmoid(gb_ref[...].astype(F32)) * yb
    mix = jnp.dot(merged.astype(BF16), wo_ref[...], preferred_element_type=F32)
    h = _layer_norm(alpha * x_ref[...] + mix, g_ref[...], b_ref[...])
    h_ref[:, :D_MODEL] = h

    logits = jnp.dot(h.astype(BF16), wr_ref[...], preferred_element_type=F32)
    lt = logits.T[:N_EXPERTS, :]
    ex = jnp.exp(lt - jnp.max(lt, axis=0, keepdims=True))
    probs = ex / jnp.sum(ex, axis=0, keepdims=True)
    sel = probs + rb_ref[...]
    sel_rows = [sel[e:e + 1, :] for e in range(N_EXPERTS)]
    prob_rows = [probs[e:e + 1, :] for e in range(N_EXPERTS)]
    scores = []
    for g in range(N_GROUPS):
        m1, _, m2, _ = _top2_of4(sel_rows[g * EXPERTS_PER_GROUP:(g + 1) * EXPERTS_PER_GROUP])
        scores.append(m1 + m2)
    best = jnp.maximum(jnp.maximum(scores[0], scores[1]), jnp.maximum(scores[2], scores[3]))
    grp = jnp.where(scores[0] == best, 0, jnp.where(scores[1] == best, 1, jnp.where(scores[2] == best, 2, 3)))
    in_grp = [_pick4(grp, [sel_rows[g * EXPERTS_PER_GROUP + r] for g in range(N_GROUPS)])
              for r in range(EXPERTS_PER_GROUP)]
    p_grp = [_pick4(grp, [prob_rows[g * EXPERTS_PER_GROUP + r] for g in range(N_GROUPS)])
             for r in range(EXPERTS_PER_GROUP)]
    _, l0, _, l1 = _top2_of4(in_grp)
    p0 = _pick4(l0, p_grp)
    p1 = _pick4(l1, p_grp)
    w0 = p0 / (p0 + p1)
    w1 = p1 / (p0 + p1)
    r128 = lax.broadcasted_iota(jnp.int32, (LANES, tm), 0)
    local_w = jnp.zeros((LANES, tm), F32)
    for r in range(EXPERTS_PER_GROUP):
        local_w = jnp.where(r128 == r, jnp.where(l0 == r, w0, jnp.where(l1 == r, w1, 0.0)), local_w)
    h_ref[:, D_MODEL:] = local_w.T

    gidx = lax.broadcasted_iota(jnp.int32, (8, tm), 0)
    hit = gidx == grp
    onehot = jnp.where(hit, 1.0, 0.0)
    trow = lax.broadcasted_iota(jnp.int32, (tm, tm), 0)
    tcol = lax.broadcasted_iota(jnp.int32, (tm, tm), 1)
    earlier = jnp.where(trow < tcol, 1.0, 0.0).astype(BF16)
    before = jnp.dot(onehot.astype(BF16), earlier, preferred_element_type=F32) + carry_ref[...]
    rank = jnp.sum(jnp.where(hit, before, 0.0), axis=0, keepdims=True).astype(jnp.int32)
    carry = carry_ref[...] + jnp.sum(onehot, axis=1, keepdims=True)
    carry_ref[...] = carry
    cnt_ref[...] = jnp.broadcast_to(carry, cnt_ref.shape).astype(jnp.int32)
    ri_ref[...] = jnp.where(gidx == 0, grp, jnp.where(gidx == 1, rank, 0))


def _merge(o_a, o_b, gates, x, w_br_a, w_br_b, w_o, ln_g, ln_b, w_router, router_bias, alpha):
    n = x.shape[0]
    tm = MERGE_TM
    full = lambda shape: pl.BlockSpec(shape, lambda t: (0,) * len(shape))
    return pl.pallas_call(
        functools.partial(_merge_kernel, tm=tm, alpha=alpha),
        grid=(n // tm,),
        in_specs=[pl.BlockSpec((tm, SB_WIDTH), lambda t: (t, 0)),
                  pl.BlockSpec((tm, DN_WIDTH), lambda t: (t, 0)),
                  pl.BlockSpec((tm, D_MODEL), lambda t: (t, 0)),
                  pl.BlockSpec((tm, D_MODEL), lambda t: (t, 1)),
                  pl.BlockSpec((tm, D_MODEL), lambda t: (t, 0)),
                  full((SB_WIDTH, D_MODEL)), full((DN_WIDTH, D_MODEL)), full((D_MODEL, D_MODEL)),
                  full((1, D_MODEL)), full((1, D_MODEL)),
                  full((D_MODEL, LANES)), full((N_EXPERTS, 1))],
        out_specs=[pl.BlockSpec((tm, H_EXT), lambda t: (t, 0)),
                   pl.BlockSpec((8, tm), lambda t: (0, t)),
                   pl.BlockSpec((8, LANES), lambda t: (0, 0))],
        out_shape=[jax.ShapeDtypeStruct((n, H_EXT), F32),
                   jax.ShapeDtypeStruct((8, n), jnp.int32),
                   jax.ShapeDtypeStruct((8, LANES), jnp.int32)],
        scratch_shapes=[pltpu.VMEM((8, 1), F32)],
        compiler_params=_params("arbitrary"),
        name="merge_ln_route",
    )(o_a, o_b, gates, gates, x, w_br_a, w_br_b, w_o, ln_g, ln_b, w_router, router_bias)


def _group_ffn_kernel(blk_grp_ref, src_ref, dst_ref, h_hbm, wg_ref, wu_ref, wd_ref, f_hbm,
                      xbuf, obuf, gsem, ssem, *, rb, n_blk):
    del blk_grp_ref
    i = pl.program_id(0)
    slot = i % 2
    other = 1 - slot

    def gather_row(block, r, s):
        return pltpu.make_async_copy(h_hbm.at[pl.ds(src_ref[block * rb + r], 1), :],
                                     xbuf.at[s, pl.ds(r, 1), :], gsem.at[s])

    def scatter_row(block, r, s):
        return pltpu.make_async_copy(obuf.at[s, pl.ds(r, 1), :],
                                     f_hbm.at[pl.ds(dst_ref[(block + 2) * rb + r], 1), :], ssem.at[s])

    @pl.when(i == 0)
    def _():
        obuf[...] = jnp.zeros_like(obuf)

        def issue(r, _):
            gather_row(0, r, 0).start()
            scatter_row(-2, r, 1).start()
            return 0

        lax.fori_loop(0, rb, issue, 0)
        pltpu.make_async_copy(obuf.at[1], f_hbm.at[pl.ds(0, rb), :], ssem.at[1]).wait()

    pltpu.make_async_copy(h_hbm.at[pl.ds(0, rb), :], xbuf.at[slot], gsem.at[slot]).wait()

    @pl.when(i > 0)
    def _():
        pltpu.make_async_copy(obuf.at[slot], f_hbm.at[pl.ds(0, rb), :], ssem.at[slot]).wait()

    @pl.when(i < n_blk)
    def _():
        x = xbuf[slot, :, :D_MODEL].astype(BF16)
        part = rb // EXPERTS_PER_GROUP
        for e in range(EXPERTS_PER_GROUP):
            for r in range(e * part, (e + 1) * part):
                gather_row(i + 1, r, other).start()
                scatter_row(i - 1, r, other).start()
            gate = jnp.dot(x, wg_ref[e], preferred_element_type=F32)
            up = jnp.dot(x, wu_ref[e], preferred_element_type=F32)
            hid = (gate * _sigmoid(gate) * up).astype(BF16)
            y = xbuf[slot, :, D_MODEL + e:D_MODEL + e + 1] * jnp.dot(hid, wd_ref[e], preferred_element_type=F32)
            if e == 0:
                obuf[slot] = y
            else:
                obuf[slot] += y

    @pl.when(i == n_blk)
    def _():
        def issue(r, _):
            scatter_row(n_blk - 1, r, other).start()
            return 0

        lax.fori_loop(0, rb, issue, 0)
        pltpu.make_async_copy(obuf.at[other], f_hbm.at[pl.ds(0, rb), :], ssem.at[other]).wait()


def _group_ffn(h_ext, blk_grp, row_src, row_dst, w_gate, w_up, w_down, n_tok):
    rb = ROUTE_BLOCK
    n_blk = row_src.shape[0] // rb - 1
    wspec = lambda a, b: pl.BlockSpec((EXPERTS_PER_GROUP, a, b), lambda i, bg, rs, rd: (bg[i], 0, 0))
    return pl.pallas_call(
        functools.partial(_group_ffn_kernel, rb=rb, n_blk=n_blk),
        grid_spec=pltpu.PrefetchScalarGridSpec(
            num_scalar_prefetch=3,
            grid=(n_blk + 1,),
            in_specs=[pl.BlockSpec(memory_space=pl.ANY),
                      wspec(D_MODEL, D_EXPERT), wspec(D_MODEL, D_EXPERT), wspec(D_EXPERT, D_MODEL)],
            out_specs=pl.BlockSpec(memory_space=pl.ANY),
            scratch_shapes=[pltpu.VMEM((2, rb, H_EXT), F32), pltpu.VMEM((2, rb, D_MODEL), F32),
                            pltpu.SemaphoreType.DMA((2,)), pltpu.SemaphoreType.DMA((2,))]),
        out_shape=jax.ShapeDtypeStruct((n_tok + 2 * rb, D_MODEL), F32),
        compiler_params=_params("arbitrary"),
        name="group_ffn",
    )(blk_grp, row_src, row_dst, h_ext, w_gate, w_up, w_down)


def _ffn_norm_kernel(h_ref, f_ref, g_ref, b_ref, o_ref, ob_ref, *, alpha):
    out = _layer_norm(alpha * h_ref[...] + f_ref[...], g_ref[...], b_ref[...])
    o_ref[...] = out
    ob_ref[...] = out.astype(BF16)


def _ffn_norm(h_ext, ffn, ln_g, ln_b, alpha):
    n = h_ext.shape[0]
    tm = NORM_TM
    row = pl.BlockSpec((tm, D_MODEL), lambda t: (t, 0))
    vec = pl.BlockSpec((1, D_MODEL), lambda t: (0, 0))
    return pl.pallas_call(
        functools.partial(_ffn_norm_kernel, alpha=alpha),
        grid=(n // tm,),
        in_specs=[row, row, vec, vec],
        out_specs=[row, row],
        out_shape=[jax.ShapeDtypeStruct((n, D_MODEL), F32), jax.ShapeDtypeStruct((n, D_MODEL), BF16)],
        compiler_params=_params("parallel"),
        name="ffn_norm",
    )(h_ext, ffn, ln_g, ln_b)


def _route_rows(ri, cnt, n_tok):
    rb = ROUTE_BLOCK
    counts = cnt[:N_GROUPS, 0]
    padded = (counts + rb - 1) // rb * rb
    pends = jnp.cumsum(padded)
    dest = (pends - padded)[ri[0]] + ri[1]
    p_rows = n_tok + N_GROUPS * rb
    row_tok = jnp.full((p_rows,), -1, jnp.int32).at[dest].set(jnp.arange(n_tok, dtype=jnp.int32))
    spare = n_tok + jnp.arange(-2 * rb, p_rows, dtype=jnp.int32) % (2 * rb)
    row_src = jnp.concatenate([jnp.maximum(row_tok, 0), jnp.zeros((rb,), jnp.int32)])
    row_dst = jnp.concatenate([jnp.full((2 * rb,), -1, jnp.int32), row_tok])
    row_dst = jnp.where(row_dst < 0, spare, row_dst)
    blk_start = jnp.arange(p_rows // rb + 1, dtype=jnp.int32) * rb
    blk_grp = jnp.minimum(jnp.searchsorted(pends, blk_start, side="right"), N_GROUPS - 1).astype(jnp.int32)
    return row_src, row_dst, blk_grp


def _lane_row(values, offset):
    return jnp.zeros((1, LANES), F32).at[0, offset:offset + values.shape[0]].set(values.astype(F32))


def kernel(x, w_in, conv_w, a_log, dt_bias, onorm_g, w_br_a, w_br_b, w_o, ln1_g, ln1_b, w_router,
           router_bias, w_gate, w_up, w_down, ln2_g, ln2_b):
    batch, seq, d = x.shape
    depth = w_in.shape[0]
    n_tok = batch * seq
    alpha = (2 * depth) ** 0.25
    a_end = 3 * SB_WIDTH
    b_end = a_end + 4 * DN_WIDTH
    s_end = b_end + 2 * DN_HEADS
    w_a = w_in[:, :, :a_end].astype(BF16)
    w_b = w_in[:, :, a_end:b_end].astype(BF16)
    w_s = jnp.pad(w_in[:, :, b_end:s_end], ((0, 0), (0, 0), (0, LANES - 2 * DN_HEADS))).astype(BF16)
    w_g = w_in[:, :, s_end:].astype(BF16)
    w_r = jnp.pad(w_router, ((0, 0), (0, LANES - N_EXPERTS))).astype(BF16)
    r_bias = router_bias.astype(F32).reshape(N_EXPERTS, 1)
    w_br_a, w_br_b, w_o = w_br_a.astype(BF16), w_br_b.astype(BF16), w_o.astype(BF16)
    w_gate, w_up, w_down = w_gate.astype(BF16), w_up.astype(BF16), w_down.astype(BF16)

    xf = x.reshape(n_tok, d)
    xb = xf.astype(BF16)
    for l in range(depth):
        pa, pc, pz, ps, gates = _project(xb, w_a[l], w_b[l], w_s[l], w_g[l], conv_w[l].astype(F32), seq)
        pa = pa.reshape(batch, seq, a_end)
        pc = pc.reshape(batch, seq, 3 * DN_WIDTH)
        pz = pz.reshape(batch, seq, DN_WIDTH)
        ps = ps.reshape(batch, seq, LANES)
        o_a = _stick_breaking(pa, batch, seq).reshape(n_tok, SB_WIDTH)
        o_b = _gated_deltanet(pc, pz, ps, _lane_row(a_log[l], DN_HEADS), _lane_row(dt_bias[l], DN_HEADS),
                              onorm_g[l].astype(F32).reshape(1, DN_HEAD_DIM), batch, seq).reshape(n_tok, DN_WIDTH)
        h_ext, ri, cnt = _merge(o_a, o_b, gates, xf, w_br_a[l], w_br_b[l], w_o[l],
                                ln1_g[l].reshape(1, d), ln1_b[l].reshape(1, d), w_r, r_bias, alpha)
        row_src, row_dst, blk_grp = _route_rows(ri, cnt, n_tok)
        ffn = _group_ffn(h_ext, blk_grp, row_src, row_dst, w_gate[l], w_up[l], w_down[l], n_tok)
        xf, xb = _ffn_norm(h_ext, ffn, ln2_g[l].reshape(1, d), ln2_b[l].reshape(1, d), alpha)
    return xf.reshape(batch, seq, d)
```

```python
import functools

import jax
import jax.numpy as jnp
from jax import lax
from jax.experimental import pallas as pl
from jax.experimental.pallas import tpu as pltpu

D_MODEL = 1024
SB_HEADS = 8
SB_HEAD_DIM = 64
SB_WIDTH = SB_HEADS * SB_HEAD_DIM
DN_HEADS = 4
DN_HEAD_DIM = 128
DN_WIDTH = DN_HEADS * DN_HEAD_DIM
CONV_K = 4
CHUNK = 64
N_EXPERTS = 16
N_GROUPS = 4
EXPERTS_PER_GROUP = N_EXPERTS // N_GROUPS
PAIRS_PER_GROUP = EXPERTS_PER_GROUP * (EXPERTS_PER_GROUP - 1) // 2
CLASS_ROWS = 32
D_EXPERT = 512
LN_EPS = 1e-5
NORM_EPS = 1e-6

LANES = 128
H_EXT = D_MODEL + LANES
VMEM_LIMIT = 48 * 1024 * 1024

PROJ_TM = 512
SB_TQ = 128
SB_TK = 256
SB_SUB = 4
GDN_TS = 512
GDN_GROUP = 8
MERGE_TM = 256
ROUTE_BLOCK = 256
NORM_TM = 512
SB_SKIP_LOG = -60.0

BF16 = jnp.bfloat16
F32 = jnp.float32
NT_DIMS = (((1,), (1,)), ((), ()))
TN_DIMS = (((0,), (0,)), ((), ()))


def _params(*semantics):
    return pltpu.CompilerParams(dimension_semantics=semantics, vmem_limit_bytes=VMEM_LIMIT)


def _sigmoid(x):
    return 1.0 / (1.0 + jnp.exp(-x))


def _softplus(x):
    return jnp.maximum(x, 0.0) + jnp.log(1.0 + jnp.exp(-jnp.abs(x)))


def _layer_norm(y, g, b):
    mu = jnp.mean(y, axis=-1, keepdims=True)
    yc = y - mu
    var = jnp.mean(yc * yc, axis=-1, keepdims=True)
    return yc * lax.rsqrt(var + LN_EPS) * g + b


def _proj_kernel(x_ref, wa_ref, wb_ref, ws_ref, wg_ref, cw_ref, oa_ref, oc_ref, oz_ref, os_ref, og_ref,
                 ubuf, *, tm, tiles_per_seq):
    i = pl.program_id(0)
    tail = 8
    qkv = 3 * DN_WIDTH

    @pl.when(i == 0)
    def _():
        ubuf[tm:tm + tail, :] = jnp.zeros((tail, qkv), F32)

    x = x_ref[...]
    oa_ref[...] = jnp.dot(x, wa_ref[...], preferred_element_type=F32).astype(oa_ref.dtype)
    pb = jnp.dot(x, wb_ref[...], preferred_element_type=F32)
    oz_ref[...] = pb[:, qkv:].astype(oz_ref.dtype)
    prev = ubuf[tm:tm + tail, :]
    ubuf[0:tail, :] = jnp.where(i % tiles_per_seq == 0, jnp.zeros_like(prev), prev)
    ubuf[tail:tail + tm, :] = pb[:, :qkv]
    os_ref[...] = jnp.dot(x, ws_ref[...], preferred_element_type=F32)
    gate_cols = wg_ref.shape[1]
    pieces = 4
    step = gate_cols // pieces
    lane_tiles = qkv // LANES
    for p in range(pieces):
        for c in range(p * lane_tiles // pieces, (p + 1) * lane_tiles // pieces):
            lo, hi = c * LANES, (c + 1) * LANES
            u = ubuf[:, lo:hi]
            conv = cw_ref[CONV_K - 1:CONV_K, lo:hi] * u[tail:, :]
            for j in range(CONV_K - 1):
                conv = conv + cw_ref[j:j + 1, lo:hi] * pltpu.roll(u, CONV_K - 1 - j, axis=0)[tail:, :]
            oc_ref[:, lo:hi] = (conv * _sigmoid(conv)).astype(oc_ref.dtype)
        og_ref[:, p * step:(p + 1) * step] = jnp.dot(
            x, wg_ref[:, p * step:(p + 1) * step], preferred_element_type=F32).astype(og_ref.dtype)


def _project(x, w_a, w_b, w_s, w_g, conv_w, seq):
    m, k = x.shape
    tm = PROJ_TM
    qkv = 3 * DN_WIDTH
    widths = (w_a.shape[1], qkv, w_b.shape[1] - qkv, w_s.shape[1], w_g.shape[1])
    dtypes = (BF16, BF16, BF16, F32, BF16)
    const = lambda a: pl.BlockSpec(a.shape, lambda i: (0, 0))
    return pl.pallas_call(
        functools.partial(_proj_kernel, tm=tm, tiles_per_seq=seq // tm),
        grid=(m // tm,),
        in_specs=[pl.BlockSpec((tm, k), lambda i: (i, 0)), const(w_a), const(w_b), const(w_s), const(w_g),
                  const(conv_w)],
        out_specs=[pl.BlockSpec((tm, n), lambda i: (i, 0)) for n in widths],
        out_shape=[jax.ShapeDtypeStruct((m, n), dt) for n, dt in zip(widths, dtypes)],
        scratch_shapes=[pltpu.VMEM((tm + 8, qkv), F32)],
        compiler_params=_params("arbitrary"),
        name="proj",
    )(x, w_a, w_b, w_s, w_g, conv_w)


def _sb_kernel(q_ref, k_ref, v_ref, later_ref, o_ref, acc_ref, c_ref, *, tq, tk, sub):
    i = pl.program_id(2)
    lane = lax.broadcasted_iota(jnp.int32, (1, LANES), 1)
    head_a = lane < SB_HEAD_DIM
    row = lax.broadcasted_iota(jnp.int32, (tq, tk), 0)
    col = lax.broadcasted_iota(jnp.int32, (tq, tk), 1)
    later = later_ref[...]
    tiles = range(sub)
    pairs = [(s, h) for s in tiles for h in range(2)]
    q_ends = [(i * sub + s + 1) * tq for s in tiles]
    q_heads = []
    for s in tiles:
        q = q_ref[s * tq:(s + 1) * tq, :] * jnp.asarray(SB_HEAD_DIM ** -0.5, q_ref.dtype)
        zero = jnp.zeros_like(q)
        q_heads.append((jnp.where(head_a, q, zero), jnp.where(head_a, zero, q)))

    def walk(b, first):
        nominal = [qe - (b + 1) * tk for qe in q_ends]
        starts = [pl.multiple_of(jnp.maximum(nm, 0), tq) for nm in nominal]
        ks = [k_ref[pl.ds(st, tk), :] for st in starts]
        vs = [v_ref[pl.ds(st, tk), :] for st in starts]
        valids = [(st + col < qe - tq + row) & (st + col < nm + tk)
                  for st, qe, nm in zip(starts, q_ends, nominal)]
        zs = [lax.dot_general(q_heads[s][h], ks[s], NT_DIMS, preferred_element_type=F32) for s, h in pairs]
        log_betas = [jnp.minimum(z, 0.0) - jnp.log(1.0 + jnp.exp(-jnp.abs(z))) for z in zs]
        log_keeps = [jnp.where(valids[s], lb - z, 0.0) for (s, _), lb, z in zip(pairs, log_betas, zs)]
        sums = [jnp.dot(lk.astype(BF16), later, preferred_element_type=F32) for lk in log_keeps]
        weights = [jnp.where(valids[s], jnp.exp(lb + sm[:, :tk]), 0.0).astype(BF16)
                   for (s, _), lb, sm in zip(pairs, log_betas, sums)]
        pvs = [jnp.dot(a, vs[s], preferred_element_type=F32) for (s, _), a in zip(pairs, weights)]
        cmax = [None] * sub
        for (s, h), pv, sm in zip(pairs, pvs, sums):
            if first:
                acc_ref[s, h] = pv
                c = sm[:, tk:]
            else:
                c = c_ref[s, h]
                acc_ref[s, h] += jnp.exp(c) * pv
                c = c + sm[:, tk:]
            c_ref[s, h] = c
            m = jnp.max(c)
            cmax[s] = m if cmax[s] is None else jnp.maximum(cmax[s], m)
        return cmax

    def unfinished(carry):
        b = carry[0]
        need = [(qe - b * tk > 0) & (cm > SB_SKIP_LOG) for qe, cm in zip(q_ends, carry[1:])]
        return functools.reduce(jnp.logical_or, need)

    lax.while_loop(unfinished, lambda carry: (carry[0] + 1, *walk(carry[0], False)),
                   (jnp.int32(1), *walk(0, True)))
    for s in tiles:
        o_ref[s * tq:(s + 1) * tq, :] = jnp.where(head_a, acc_ref[s, 0], acc_ref[s, 1]).astype(o_ref.dtype)


def _stick_breaking(qkv, batch, seq):
    pairs = SB_WIDTH // LANES
    tq, tk, sub = SB_TQ, SB_TK, SB_SUB
    rows = tq * sub
    r = jnp.arange(tk)[:, None]
    c = jnp.arange(tk + LANES)[None, :]
    later = ((c >= tk) | (r > c)).astype(BF16)
    return pl.pallas_call(
        functools.partial(_sb_kernel, tq=tq, tk=tk, sub=sub),
        grid=(batch, pairs, seq // rows),
        in_specs=[pl.BlockSpec((None, rows, LANES), lambda b, p, i: (b, i, p)),
                  pl.BlockSpec((None, seq, LANES), lambda b, p, i: (b, 0, pairs + p)),
                  pl.BlockSpec((None, seq, LANES), lambda b, p, i: (b, 0, 2 * pairs + p)),
                  pl.BlockSpec((tk, tk + LANES), lambda b, p, i: (0, 0))],
        out_specs=pl.BlockSpec((None, rows, LANES), lambda b, p, i: (b, i, p)),
        out_shape=jax.ShapeDtypeStruct((batch, seq, SB_WIDTH), BF16),
        scratch_shapes=[pltpu.VMEM((sub, 2, tq, LANES), F32), pltpu.VMEM((sub, 2, tq, LANES), F32)],
        compiler_params=_params("parallel", "parallel", "parallel"),
        name="stick_breaking",
    )(qkv, qkv, qkv, later)


CAT = DN_HEADS * CHUNK


def _split3(x):
    hi = x.astype(BF16)
    r = x - hi.astype(F32)
    mid = r.astype(BF16)
    return hi, mid, (r - mid.astype(F32)).astype(BF16)


def _block_diag(x, mask):
    return jnp.concatenate([x] * DN_HEADS, axis=0) * mask


def _cat_matmul(l, r, mask):
    return jnp.dot(l.astype(BF16), _block_diag(r.astype(BF16), mask), preferred_element_type=F32)


def _by_head(head, vals):
    return jnp.where(head == 0, vals[0], jnp.where(head == 1, vals[1], jnp.where(head == 2, vals[2], vals[3])))


def _gdn_constants():
    i = jnp.arange(CHUNK)
    tri = (i[:, None] >= i[None, :]).astype(BF16)
    row_head = jnp.arange(CAT)[:, None] // CHUNK
    wide_head = jnp.arange(DN_WIDTH) // DN_HEAD_DIM
    return (jnp.concatenate([tri] * 3, axis=1),
            (row_head == jnp.arange(CAT)[None, :] // CHUNK).astype(BF16),
            (row_head == wide_head[None, :]).astype(BF16),
            (row_head == jnp.concatenate([wide_head, wide_head])[None, :]).astype(BF16))


def _gdn_kernel(qkv_ref, z_ref, s_ref, alog_ref, dt_ref, og_ref, tri3_ref, bd_ref, bdk_ref,
                bdr_ref, o_ref, state_ref, sbd_ref, *, ts):
    @pl.when(pl.program_id(1) == 0)
    def _():
        state_ref[...] = jnp.zeros_like(state_ref)
        sbd_ref[...] = jnp.zeros_like(sbd_ref)

    rows = lax.broadcasted_iota(jnp.int32, (CHUNK, CAT), 0)
    lane = lax.broadcasted_iota(jnp.int32, (CHUNK, CAT), 1)
    cols = lane % CHUNK
    head = lane // CHUNK
    incl = rows >= cols
    strict = rows > cols
    upper = rows <= cols
    pair = rows // 2 == cols // 2
    lower_blocks = []
    blk = 2
    while blk < CHUNK:
        lower_blocks.append((rows // (2 * blk) == cols // (2 * blk)) & (rows // blk != cols // blk))
        blk *= 2
    neg_a = -jnp.exp(alog_ref[...])
    dt_bias = dt_ref[...]
    onorm_g = og_ref[...]
    heads = range(DN_HEADS)

    def operands(r0, beta_all, gcol):
        qs, ks, q_decs, k_decs, rhs_v, rhs_k, lasts = [], [], [], [], [], [], []
        for h in heads:
            lo, hi = h * DN_HEAD_DIM, (h + 1) * DN_HEAD_DIM
            q = qkv_ref[pl.ds(r0, CHUNK), lo:hi].astype(F32)
            k = qkv_ref[pl.ds(r0, CHUNK), DN_WIDTH + lo:DN_WIDTH + hi].astype(F32)
            v = qkv_ref[pl.ds(r0, CHUNK), 2 * DN_WIDTH + lo:2 * DN_WIDTH + hi].astype(F32)
            q = q * lax.rsqrt(jnp.sum(q * q, axis=-1, keepdims=True) + NORM_EPS) * (DN_HEAD_DIM ** -0.5)
            k = k * lax.rsqrt(jnp.sum(k * k, axis=-1, keepdims=True) + NORM_EPS)
            beta = beta_all[:, h:h + 1]
            gc = gcol[:, h * CHUNK:h * CHUNK + 1]
            glast = gc[CHUNK - 1:CHUNK, :]
            eg = jnp.exp(gc)
            qs.append(q)
            ks.append(k)
            q_decs.append(q * eg)
            k_decs.append((k * jnp.exp(glast - gc)).astype(BF16))
            rhs_v.append(beta * v)
            rhs_k.append(beta * eg * k)
            lasts.append(jnp.exp(glast))
        return (jnp.concatenate(ks, axis=1).astype(BF16), jnp.concatenate(qs, axis=1).astype(BF16),
                jnp.concatenate(rhs_v + rhs_k, axis=1), jnp.concatenate(q_decs, axis=1).astype(BF16),
                k_decs, lasts)

    def group_body(gi, _):
        group = range(GDN_GROUP)
        r0s = [pl.multiple_of((gi * GDN_GROUP + j) * CHUNK, CHUNK) for j in group]
        sms = [s_ref[pl.ds(r0, CHUNK), :] for r0 in r0s]
        beta_alls = [_sigmoid(sm) for sm in sms]
        g_alls = [neg_a * _softplus(sm + dt_bias) for sm in sms]
        beta_cats = [_by_head(head, [b[:, h:h + 1] for h in heads]) for b in beta_alls]
        g_cats = [_by_head(head, [g[:, DN_HEADS + h:DN_HEADS + h + 1] for h in heads]) for g in g_alls]
        gcols = [jnp.dot(tri3_ref[...], jnp.concatenate(_split3(g), axis=0), preferred_element_type=F32)
                 for g in g_cats]
        grows = [jnp.sum(jnp.where(upper, g, 0.0), axis=0, keepdims=True) for g in g_cats]
        gammas = [jnp.exp(jnp.where(incl, gc - gr, -jnp.inf)) for gc, gr in zip(gcols, grows)]
        ops = [operands(r0, b, gc) for r0, b, gc in zip(r0s, beta_alls, gcols)]
        bdk = bdk_ref[...]
        kqs = [lax.dot_general(jnp.concatenate([kc, qc], axis=0), _block_diag(kc, bdk), NT_DIMS,
                               preferred_element_type=F32) for kc, qc, _, _, _, _ in ops]
        qks = [(kq[CHUNK:] * gm).astype(BF16) for kq, gm in zip(kqs, gammas)]
        a_lows = [jnp.where(strict, b * kq[:CHUNK] * gm, 0.0) for b, kq, gm in zip(beta_cats, kqs, gammas)]
        bd = bd_ref[...]
        ys = [-jnp.where(pair, a, 0.0) for a in a_lows]
        for lower in lower_blocks:
            lows = [jnp.where(lower, a, 0.0) for a in a_lows]
            ts_ = [low + _cat_matmul(low, y, bd) for low, y in zip(lows, ys)]
            ys = [y - t - _cat_matmul(y, t, bd) for y, t in zip(ys, ts_)]
        bdr = bdr_ref[...]
        uws = [op[2] + _cat_matmul(y, op[2], bdr) for y, op in zip(ys, ops)]
        for j in group:
            _, _, _, q_dec, k_decs, lasts = ops[j]
            u, w = uws[j][:, :DN_WIDTH], uws[j][:, DN_WIDTH:]
            ws = jnp.dot(jnp.concatenate([w.astype(BF16), q_dec], axis=0), sbd_ref[...],
                         preferred_element_type=F32)
            vb = (u - ws[:CHUNK]).astype(BF16)
            o = ws[CHUNK:] + jnp.dot(qks[j], _block_diag(vb, bdk), preferred_element_type=F32)
            for h in heads:
                lo, hi = h * DN_HEAD_DIM, (h + 1) * DN_HEAD_DIM
                new = lasts[h] * state_ref[h] + lax.dot_general(k_decs[h], vb[:, lo:hi], TN_DIMS,
                                                                preferred_element_type=F32)
                state_ref[h] = new
                sbd_ref[lo:hi, lo:hi] = new.astype(BF16)
                oh = o[:, lo:hi]
                oh = oh * lax.rsqrt(jnp.mean(oh * oh, axis=-1, keepdims=True) + NORM_EPS) * onorm_g
                zz = z_ref[pl.ds(r0s[j], CHUNK), lo:hi].astype(F32)
                o_ref[pl.ds(r0s[j], CHUNK), lo:hi] = (oh * (zz * _sigmoid(zz))).astype(o_ref.dtype)
        return 0

    lax.fori_loop(0, ts // (CHUNK * GDN_GROUP), group_body, 0)


def _gated_deltanet(pc, pz, ps, alog_row, dt_row, onorm_row, batch, seq):
    ts = GDN_TS
    consts = _gdn_constants()
    const = lambda a: pl.BlockSpec(a.shape, lambda b, s: (0, 0))
    tile = lambda width: pl.BlockSpec((None, ts, width), lambda b, s: (b, s, 0))
    return pl.pallas_call(
        functools.partial(_gdn_kernel, ts=ts),
        grid=(batch, seq // ts),
        in_specs=[tile(3 * DN_WIDTH), tile(DN_WIDTH), tile(LANES), const(alog_row), const(dt_row),
                  const(onorm_row)] + [const(a) for a in consts],
        out_specs=tile(DN_WIDTH),
        out_shape=jax.ShapeDtypeStruct((batch, seq, DN_WIDTH), BF16),
        scratch_shapes=[pltpu.VMEM((DN_HEADS, DN_HEAD_DIM, DN_HEAD_DIM), F32),
                        pltpu.VMEM((DN_WIDTH, DN_WIDTH), BF16)],
        compiler_params=_params("parallel", "arbitrary"),
        name="gated_deltanet",
    )(pc, pz, ps, alog_row, dt_row, onorm_row, *consts)


def _top2_of4(vals):
    m1 = jnp.maximum(jnp.maximum(vals[0], vals[1]), jnp.maximum(vals[2], vals[3]))
    i1 = jnp.where(vals[0] == m1, 0, jnp.where(vals[1] == m1, 1, jnp.where(vals[2] == m1, 2, 3)))
    rest = [jnp.where(i1 == r, -jnp.inf, vals[r]) for r in range(4)]
    m2 = jnp.maximum(jnp.maximum(rest[0], rest[1]), jnp.maximum(rest[2], rest[3]))
    i2 = jnp.where(rest[0] == m2, 0, jnp.where(rest[1] == m2, 1, jnp.where(rest[2] == m2, 2, 3)))
    return m1, i1, m2, i2


def _pick4(idx, vals):
    return jnp.where(idx == 0, vals[0], jnp.where(idx == 1, vals[1], jnp.where(idx == 2, vals[2], vals[3])))


def _merge_kernel(oa_ref, ob_ref, ga_ref, gb_ref, x_ref, wa_ref, wb_ref, wo_ref, g_ref, b_ref,
                  wr_ref, rb_ref, h_ref, ri_ref, cnt_ref, carry_ref, *, tm, alpha):
    t = pl.program_id(0)

    @pl.when(t == 0)
    def _():
        carry_ref[...] = jnp.zeros_like(carry_ref)

    ya = jnp.dot(oa_ref[...], wa_ref[...], preferred_element_type=F32)
    yb = jnp.dot(ob_ref[...], wb_ref[...], preferred_element_type=F32)
    merged = _sigmoid(ga_ref[...].astype(F32)) * ya + _sigmoid(gb_ref[...].astype(F32)) * yb
    mix = jnp.dot(merged.astype(BF16), wo_ref[...], preferred_element_type=F32)
    h = _layer_norm(alpha * x_ref[...] + mix, g_ref[...], b_ref[...])
    h_ref[:, :D_MODEL] = h

    logits = jnp.dot(h.astype(BF16), wr_ref[...], preferred_element_type=F32)
    lt = logits.T[:N_EXPERTS, :]
    ex = jnp.exp(lt - jnp.max(lt, axis=0, keepdims=True))
    probs = ex / jnp.sum(ex, axis=0, keepdims=True)
    sel = probs + rb_ref[...]
    sel_rows = [sel[e:e + 1, :] for e in range(N_EXPERTS)]
    prob_rows = [probs[e:e + 1, :] for e in range(N_EXPERTS)]
    scores = []
    for g in range(N_GROUPS):
        m1, _, m2, _ = _top2_of4(sel_rows[g * EXPERTS_PER_GROUP:(g + 1) * EXPERTS_PER_GROUP])
        scores.append(m1 + m2)
    best = jnp.maximum(jnp.maximum(scores[0], scores[1]), jnp.maximum(scores[2], scores[3]))
    grp = jnp.where(scores[0] == best, 0, jnp.where(scores[1] == best, 1, jnp.where(scores[2] == best, 2, 3)))
    in_grp = [_pick4(grp, [sel_rows[g * EXPERTS_PER_GROUP + r] for g in range(N_GROUPS)])
              for r in range(EXPERTS_PER_GROUP)]
    p_grp = [_pick4(grp, [prob_rows[g * EXPERTS_PER_GROUP + r] for g in range(N_GROUPS)])
             for r in range(EXPERTS_PER_GROUP)]
    _, l0, _, l1 = _top2_of4(in_grp)
    p0 = _pick4(l0, p_grp)
    p1 = _pick4(l1, p_grp)
    w0 = p0 / (p0 + p1)
    w1 = p1 / (p0 + p1)
    r128 = lax.broadcasted_iota(jnp.int32, (LANES, tm), 0)
    local_w = jnp.zeros((LANES, tm), F32)
    for r in range(EXPERTS_PER_GROUP):
        local_w = jnp.where(r128 == r, jnp.where(l0 == r, w0, jnp.where(l1 == r, w1, 0.0)), local_w)
    h_ref[:, D_MODEL:] = local_w.T

    first, second = jnp.minimum(l0, l1), jnp.maximum(l0, l1)
    pair_id = jnp.where(first == 0, second - 1, jnp.where(first == 1, second + 1, PAIRS_PER_GROUP - 1))
    cls = grp * PAIRS_PER_GROUP + pair_id
    cidx = lax.broadcasted_iota(jnp.int32, (CLASS_ROWS, tm), 0)
    hit = cidx == cls
    onehot = jnp.where(hit, 1.0, 0.0)
    trow = lax.broadcasted_iota(jnp.int32, (tm, tm), 0)
    tcol = lax.broadcasted_iota(jnp.int32, (tm, tm), 1)
    earlier = jnp.where(trow < tcol, 1.0, 0.0).astype(BF16)
    before = jnp.dot(onehot.astype(BF16), earlier, preferred_element_type=F32) + carry_ref[...]
    rank = jnp.sum(jnp.where(hit, before, 0.0), axis=0, keepdims=True).astype(jnp.int32)
    carry = carry_ref[...] + jnp.sum(onehot, axis=1, keepdims=True)
    carry_ref[...] = carry
    cnt_ref[...] = jnp.broadcast_to(carry, cnt_ref.shape).astype(jnp.int32)
    r8 = lax.broadcasted_iota(jnp.int32, (8, tm), 0)
    ri_ref[...] = jnp.where(r8 == 0, cls, jnp.where(r8 == 1, rank, 0))


def _merge(o_a, o_b, gates, x, w_br_a, w_br_b, w_o, ln_g, ln_b, w_router, router_bias, alpha):
    n = x.shape[0]
    tm = MERGE_TM
    full = lambda shape: pl.BlockSpec(shape, lambda t: (0,) * len(shape))
    return pl.pallas_call(
        functools.partial(_merge_kernel, tm=tm, alpha=alpha),
        grid=(n // tm,),
        in_specs=[pl.BlockSpec((tm, SB_WIDTH), lambda t: (t, 0)),
                  pl.BlockSpec((tm, DN_WIDTH), lambda t: (t, 0)),
                  pl.BlockSpec((tm, D_MODEL), lambda t: (t, 0)),
                  pl.BlockSpec((tm, D_MODEL), lambda t: (t, 1)),
                  pl.BlockSpec((tm, D_MODEL), lambda t: (t, 0)),
                  full((SB_WIDTH, D_MODEL)), full((DN_WIDTH, D_MODEL)), full((D_MODEL, D_MODEL)),
                  full((1, D_MODEL)), full((1, D_MODEL)),
                  full((D_MODEL, LANES)), full((N_EXPERTS, 1))],
        out_specs=[pl.BlockSpec((tm, H_EXT), lambda t: (t, 0)),
                   pl.BlockSpec((8, tm), lambda t: (0, t)),
                   pl.BlockSpec((CLASS_ROWS, LANES), lambda t: (0, 0))],
        out_shape=[jax.ShapeDtypeStruct((n, H_EXT), F32),
                   jax.ShapeDtypeStruct((8, n), jnp.int32),
                   jax.ShapeDtypeStruct((CLASS_ROWS, LANES), jnp.int32)],
        scratch_shapes=[pltpu.VMEM((CLASS_ROWS, 1), F32)],
        compiler_params=_params("arbitrary"),
        name="merge_ln_route",
    )(o_a, o_b, gates, gates, x, w_br_a, w_br_b, w_o, ln_g, ln_b, w_router, router_bias)


def _group_ffn_kernel(blk_grp_ref, src_ref, dst_ref, used_ref, n_used_ref, h_hbm, wg_ref, wu_ref, wd_ref, f_hbm,
                      xbuf, obuf, gsem, ssem, *, rb, n_blk):
    del blk_grp_ref
    i = pl.program_id(0)
    slot = i % 2
    other = 1 - slot

    def gather_row(block, r, s):
        return pltpu.make_async_copy(h_hbm.at[pl.ds(src_ref[block * rb + r], 1), :],
                                     xbuf.at[s, pl.ds(r, 1), :], gsem.at[s])

    def scatter_row(block, r, s):
        return pltpu.make_async_copy(obuf.at[s, pl.ds(r, 1), :],
                                     f_hbm.at[pl.ds(dst_ref[(block + 2) * rb + r], 1), :], ssem.at[s])

    @pl.when(i == 0)
    def _():
        obuf[...] = jnp.zeros_like(obuf)

        def issue(r, _):
            gather_row(0, r, 0).start()
            scatter_row(-2, r, 1).start()
            return 0

        lax.fori_loop(0, rb, issue, 0)
        pltpu.make_async_copy(obuf.at[1], f_hbm.at[pl.ds(0, rb), :], ssem.at[1]).wait()

    pltpu.make_async_copy(h_hbm.at[pl.ds(0, rb), :], xbuf.at[slot], gsem.at[slot]).wait()

    @pl.when(i > 0)
    def _():
        pltpu.make_async_copy(obuf.at[slot], f_hbm.at[pl.ds(0, rb), :], ssem.at[slot]).wait()

    @pl.when(i < n_blk)
    def _():
        x = xbuf[slot, :, :D_MODEL].astype(BF16)
        w_cols = [xbuf[slot, :, D_MODEL + e:D_MODEL + e + 1] for e in range(EXPERTS_PER_GROUP)]

        def apply(e):
            gate = jnp.dot(x, wg_ref[e], preferred_element_type=F32)
            up = jnp.dot(x, wu_ref[e], preferred_element_type=F32)
            hid = (gate * _sigmoid(gate) * up).astype(BF16)
            return w_cols[e] * jnp.dot(hid, wd_ref[e], preferred_element_type=F32)

        part = rb // 2
        for k in range(EXPERTS_PER_GROUP):
            for e in range(k, min(k + 3, EXPERTS_PER_GROUP)):
                @pl.when((used_ref[i * EXPERTS_PER_GROUP + k] == e) & (n_used_ref[i] > k))
                def _():
                    if k < 2:
                        for r in range(k * part, (k + 1) * part):
                            gather_row(i + 1, r, other).start(priority=0)
                            scatter_row(i - 1, r, other).start(priority=1)
                    if k == 0:
                        obuf[slot] = apply(e)
                    else:
                        obuf[slot] += apply(e)

    @pl.when(i == n_blk)
    def _():
        def issue(r, _):
            scatter_row(n_blk - 1, r, other).start()
            return 0

        lax.fori_loop(0, rb, issue, 0)
        pltpu.make_async_copy(obuf.at[other], f_hbm.at[pl.ds(0, rb), :], ssem.at[other]).wait()


def _group_ffn(h_ext, blk_grp, row_src, row_dst, used, n_used, w_gate, w_up, w_down, n_tok):
    rb = ROUTE_BLOCK
    n_blk = row_src.shape[0] // rb - 1
    wspec = lambda a, b: pl.BlockSpec((EXPERTS_PER_GROUP, a, b), lambda i, bg, rs, rd, us, nu: (bg[i], 0, 0))
    return pl.pallas_call(
        functools.partial(_group_ffn_kernel, rb=rb, n_blk=n_blk),
        grid_spec=pltpu.PrefetchScalarGridSpec(
            num_scalar_prefetch=5,
            grid=(n_blk + 1,),
            in_specs=[pl.BlockSpec(memory_space=pl.ANY),
                      wspec(D_MODEL, D_EXPERT), wspec(D_MODEL, D_EXPERT), wspec(D_EXPERT, D_MODEL)],
            out_specs=pl.BlockSpec(memory_space=pl.ANY),
            scratch_shapes=[pltpu.VMEM((2, rb, H_EXT), F32), pltpu.VMEM((2, rb, D_MODEL), F32),
                            pltpu.SemaphoreType.DMA((2,)), pltpu.SemaphoreType.DMA((2,))]),
        out_shape=jax.ShapeDtypeStruct((n_tok + 2 * rb, D_MODEL), F32),
        compiler_params=_params("arbitrary"),
        name="group_ffn",
    )(blk_grp, row_src, row_dst, used, n_used, h_ext, w_gate, w_up, w_down)


def _ffn_norm_kernel(h_ref, f_ref, g_ref, b_ref, o_ref, ob_ref, *, alpha):
    out = _layer_norm(alpha * h_ref[...] + f_ref[...], g_ref[...], b_ref[...])
    o_ref[...] = out
    ob_ref[...] = out.astype(BF16)


def _ffn_norm(h_ext, ffn, ln_g, ln_b, alpha):
    n = h_ext.shape[0]
    tm = NORM_TM
    row = pl.BlockSpec((tm, D_MODEL), lambda t: (t, 0))
    vec = pl.BlockSpec((1, D_MODEL), lambda t: (0, 0))
    return pl.pallas_call(
        functools.partial(_ffn_norm_kernel, alpha=alpha),
        grid=(n // tm,),
        in_specs=[row, row, vec, vec],
        out_specs=[row, row],
        out_shape=[jax.ShapeDtypeStruct((n, D_MODEL), F32), jax.ShapeDtypeStruct((n, D_MODEL), BF16)],
        compiler_params=_params("parallel"),
        name="ffn_norm",
    )(h_ext, ffn, ln_g, ln_b)


def _route_rows(ri, cnt, n_tok):
    rb = ROUTE_BLOCK
    n_cls = N_GROUPS * PAIRS_PER_GROUP
    c_count = cnt[:n_cls, 0].reshape(N_GROUPS, PAIRS_PER_GROUP)
    g_count = jnp.sum(c_count, axis=1)
    g_padded = (g_count + rb - 1) // rb * rb
    g_end = jnp.cumsum(g_padded)
    c_start = ((g_end - g_padded)[:, None] + jnp.cumsum(c_count, axis=1) - c_count).reshape(n_cls)
    c_count = c_count.reshape(n_cls)
    dest = c_start[ri[0]] + ri[1]
    p_rows = n_tok + N_GROUPS * rb
    row_tok = jnp.full((p_rows,), -1, jnp.int32).at[dest].set(jnp.arange(n_tok, dtype=jnp.int32))
    spare = n_tok + jnp.arange(-2 * rb, p_rows, dtype=jnp.int32) % (2 * rb)
    row_src = jnp.concatenate([jnp.maximum(row_tok, 0), jnp.zeros((rb,), jnp.int32)])
    row_dst = jnp.concatenate([jnp.full((2 * rb,), -1, jnp.int32), row_tok])
    row_dst = jnp.where(row_dst < 0, spare, row_dst)
    blk_start = jnp.arange(p_rows // rb + 1, dtype=jnp.int32) * rb
    blk_grp = jnp.minimum(jnp.searchsorted(g_end, blk_start, side="right"), N_GROUPS - 1).astype(jnp.int32)
    pairs = [(a, b) for a in range(EXPERTS_PER_GROUP) for b in range(a + 1, EXPERTS_PER_GROUP)]
    member = jnp.array([[e in pairs[c % PAIRS_PER_GROUP] for e in range(EXPERTS_PER_GROUP)]
                        for c in range(n_cls)])
    overlap = ((c_start[None, :] < blk_start[:, None] + rb) & ((c_start + c_count)[None, :] > blk_start[:, None])
               & (c_count[None, :] > 0))
    need = jnp.any(overlap[:, :, None] & member[None, :, :], axis=1)
    used = jnp.argsort(~need, axis=1, stable=True).astype(jnp.int32)
    n_used = jnp.maximum(jnp.sum(need, axis=1), 2).astype(jnp.int32)
    return row_src, row_dst, blk_grp, used.reshape(-1), n_used


def _lane_row(values, offset):
    return jnp.zeros((1, LANES), F32).at[0, offset:offset + values.shape[0]].set(values.astype(F32))


def kernel(x, w_in, conv_w, a_log, dt_bias, onorm_g, w_br_a, w_br_b, w_o, ln1_g, ln1_b, w_router,
           router_bias, w_gate, w_up, w_down, ln2_g, ln2_b):
    batch, seq, d = x.shape
    depth = w_in.shape[0]
    n_tok = batch * seq
    alpha = (2 * depth) ** 0.25
    a_end = 3 * SB_WIDTH
    b_end = a_end + 4 * DN_WIDTH
    s_end = b_end + 2 * DN_HEADS
    w_a = w_in[:, :, :a_end].astype(BF16)
    w_b = w_in[:, :, a_end:b_end].astype(BF16)
    w_s = jnp.pad(w_in[:, :, b_end:s_end], ((0, 0), (0, 0), (0, LANES - 2 * DN_HEADS))).astype(BF16)
    w_g = w_in[:, :, s_end:].astype(BF16)
    w_r = jnp.pad(w_router, ((0, 0), (0, LANES - N_EXPERTS))).astype(BF16)
    r_bias = router_bias.astype(F32).reshape(N_EXPERTS, 1)
    w_br_a, w_br_b, w_o = w_br_a.astype(BF16), w_br_b.astype(BF16), w_o.astype(BF16)
    w_gate, w_up, w_down = w_gate.astype(BF16), w_up.astype(BF16), w_down.astype(BF16)

    xf = x.reshape(n_tok, d)
    xb = xf.astype(BF16)
    for l in range(depth):
        pa, pc, pz, ps, gates = _project(xb, w_a[l], w_b[l], w_s[l], w_g[l], conv_w[l].astype(F32), seq)
        pa = pa.reshape(batch, seq, a_end)
        pc = pc.reshape(batch, seq, 3 * DN_WIDTH)
        pz = pz.reshape(batch, seq, DN_WIDTH)
        ps = ps.reshape(batch, seq, LANES)
        o_a = _stick_breaking(pa, batch, seq).reshape(n_tok, SB_WIDTH)
        o_b = _gated_deltanet(pc, pz, ps, _lane_row(a_log[l], DN_HEADS), _lane_row(dt_bias[l], DN_HEADS),
                              onorm_g[l].astype(F32).reshape(1, DN_HEAD_DIM), batch, seq).reshape(n_tok, DN_WIDTH)
        h_ext, ri, cnt = _merge(o_a, o_b, gates, xf, w_br_a[l], w_br_b[l], w_o[l],
                                ln1_g[l].reshape(1, d), ln1_b[l].reshape(1, d), w_r, r_bias, alpha)
        row_src, row_dst, blk_grp, used, n_used = _route_rows(ri, cnt, n_tok)
        ffn = _group_ffn(h_ext, blk_grp, row_src, row_dst, used, n_used, w_gate[l], w_up[l], w_down[l], n_tok)
        xf, xb = _ffn_norm(h_ext, ffn, ln2_g[l].reshape(1, d), ln2_b[l].reshape(1, d), alpha)
    return xf.reshape(batch, seq, d)
```

```python
import functools

import jax
import jax.numpy as jnp
from jax import lax
from jax.experimental import pallas as pl
from jax.experimental.pallas import tpu as pltpu

D_MODEL = 1024
SB_HEADS = 8
SB_HEAD_DIM = 64
SB_WIDTH = SB_HEADS * SB_HEAD_DIM
DN_HEADS = 4
DN_HEAD_DIM = 128
DN_WIDTH = DN_HEADS * DN_HEAD_DIM
CONV_K = 4
CHUNK = 64
N_EXPERTS = 16
N_GROUPS = 4
EXPERTS_PER_GROUP = N_EXPERTS // N_GROUPS
PAIRS_PER_GROUP = EXPERTS_PER_GROUP * (EXPERTS_PER_GROUP - 1) // 2
CLASS_ROWS = 32
D_EXPERT = 512
LN_EPS = 1e-5
NORM_EPS = 1e-6

LANES = 128
H_EXT = D_MODEL + LANES
VMEM_LIMIT = 48 * 1024 * 1024

PROJ_TM = 512
SB_TQ = 128
SB_TK = 256
SB_SUB = 4
GDN_TS = 512
GDN_GROUP = 8
MERGE_TM = 512
MERGE_PARTS = 2
ROUTE_BLOCK = 256
NORM_TM = 512
SB_SKIP_LOG = -60.0

BF16 = jnp.bfloat16
F32 = jnp.float32
NT_DIMS = (((1,), (1,)), ((), ()))
TN_DIMS = (((0,), (0,)), ((), ()))


def _params(*semantics):
    return pltpu.CompilerParams(dimension_semantics=semantics, vmem_limit_bytes=VMEM_LIMIT)


def _sigmoid(x):
    return 1.0 / (1.0 + jnp.exp(-x))


def _softplus(x):
    return jnp.maximum(x, 0.0) + jnp.log(1.0 + jnp.exp(-jnp.abs(x)))


def _layer_norm(y, g, b):
    mu = jnp.mean(y, axis=-1, keepdims=True)
    yc = y - mu
    var = jnp.mean(yc * yc, axis=-1, keepdims=True)
    return yc * lax.rsqrt(var + LN_EPS) * g + b


def _proj_kernel(x_ref, wa_ref, wb_ref, ws_ref, wg_ref, cw_ref, oa_ref, oc_ref, oz_ref, os_ref, og_ref,
                 ubuf, *, tm, tiles_per_seq):
    i = pl.program_id(0)
    tail = 8
    qkv = 3 * DN_WIDTH

    @pl.when(i == 0)
    def _():
        ubuf[tm:tm + tail, :] = jnp.zeros((tail, qkv), F32)

    x = x_ref[...]
    oa_ref[...] = jnp.dot(x, wa_ref[...], preferred_element_type=F32).astype(oa_ref.dtype)
    pb = jnp.dot(x, wb_ref[...], preferred_element_type=F32)
    oz_ref[...] = pb[:, qkv:].astype(oz_ref.dtype)
    prev = ubuf[tm:tm + tail, :]
    ubuf[0:tail, :] = jnp.where(i % tiles_per_seq == 0, jnp.zeros_like(prev), prev)
    ubuf[tail:tail + tm, :] = pb[:, :qkv]
    os_ref[...] = jnp.dot(x, ws_ref[...], preferred_element_type=F32)
    gate_cols = wg_ref.shape[1]
    pieces = 4
    step = gate_cols // pieces
    lane_tiles = qkv // LANES
    for p in range(pieces):
        for c in range(p * lane_tiles // pieces, (p + 1) * lane_tiles // pieces):
            lo, hi = c * LANES, (c + 1) * LANES
            u = ubuf[:, lo:hi]
            conv = cw_ref[CONV_K - 1:CONV_K, lo:hi] * u[tail:, :]
            for j in range(CONV_K - 1):
                conv = conv + cw_ref[j:j + 1, lo:hi] * pltpu.roll(u, CONV_K - 1 - j, axis=0)[tail:, :]
            oc_ref[:, lo:hi] = (conv * _sigmoid(conv)).astype(oc_ref.dtype)
        og_ref[:, p * step:(p + 1) * step] = jnp.dot(
            x, wg_ref[:, p * step:(p + 1) * step], preferred_element_type=F32).astype(og_ref.dtype)


def _project(x, w_a, w_b, w_s, w_g, conv_w, seq):
    m, k = x.shape
    tm = PROJ_TM
    qkv = 3 * DN_WIDTH
    widths = (w_a.shape[1], qkv, w_b.shape[1] - qkv, w_s.shape[1], w_g.shape[1])
    dtypes = (BF16, BF16, BF16, F32, BF16)
    const = lambda a: pl.BlockSpec(a.shape, lambda i: (0, 0))
    return pl.pallas_call(
        functools.partial(_proj_kernel, tm=tm, tiles_per_seq=seq // tm),
        grid=(m // tm,),
        in_specs=[pl.BlockSpec((tm, k), lambda i: (i, 0)), const(w_a), const(w_b), const(w_s), const(w_g),
                  const(conv_w)],
        out_specs=[pl.BlockSpec((tm, n), lambda i: (i, 0)) for n in widths],
        out_shape=[jax.ShapeDtypeStruct((m, n), dt) for n, dt in zip(widths, dtypes)],
        scratch_shapes=[pltpu.VMEM((tm + 8, qkv), F32)],
        compiler_params=_params("arbitrary"),
        name="proj",
    )(x, w_a, w_b, w_s, w_g, conv_w)


def _sb_kernel(q_ref, k_ref, v_ref, later_ref, o_ref, acc_ref, c_ref, *, tq, tk, sub):
    i = pl.program_id(2)
    lane = lax.broadcasted_iota(jnp.int32, (1, LANES), 1)
    head_a = lane < SB_HEAD_DIM
    row1 = lax.broadcasted_iota(jnp.int32, (tq, 1), 0)
    col = lax.broadcasted_iota(jnp.int32, (tq, tk), 1)
    col_b = col.astype(BF16)
    later = later_ref[...]
    tiles = range(sub)
    pairs = [(s, h) for s in tiles for h in range(2)]
    q_ends = [(i * sub + s + 1) * tq for s in tiles]
    q_heads = []
    for s in tiles:
        q = q_ref[s * tq:(s + 1) * tq, :] * jnp.asarray(SB_HEAD_DIM ** -0.5, q_ref.dtype)
        zero = jnp.zeros_like(q)
        q_heads.append((jnp.where(head_a, q, zero), jnp.where(head_a, zero, q)))

    def walk(b, first):
        nominal = [qe - (b + 1) * tk for qe in q_ends]
        starts = [pl.multiple_of(jnp.maximum(nm, 0), tq) for nm in nominal]
        ks = [k_ref[pl.ds(st, tk), :] for st in starts]
        vs = [v_ref[pl.ds(st, tk), :] for st in starts]
        bounds = [jnp.minimum(qe - tq + row1 - st, nm + tk - st) for st, qe, nm in zip(starts, q_ends, nominal)]
        valids = [col < bd for bd in bounds]
        valids_b = [col_b < bd.astype(BF16) for bd in bounds]
        zs = [lax.dot_general(q_heads[s][h], ks[s], NT_DIMS, preferred_element_type=F32) for s, h in pairs]
        zbs = [z.astype(BF16) for z in zs]
        log_betas = [jnp.minimum(zb, 0) - jnp.log(1 + jnp.exp(-jnp.abs(zb))) for zb in zbs]
        log_keeps = [jnp.where(valids_b[s], lb - zb, 0) for (s, _), lb, zb in zip(pairs, log_betas, zbs)]
        sums = [jnp.dot(lk, later, preferred_element_type=F32) for lk in log_keeps]
        weights = [jnp.where(valids[s], jnp.exp(lb.astype(F32) + sm), 0.0).astype(BF16)
                   for (s, _), lb, sm in zip(pairs, log_betas, sums)]
        totals = [sm[:, 0:1] + lk[:, 0:1].astype(F32) for sm, lk in zip(sums, log_keeps)]
        pvs = [jnp.dot(a, vs[s], preferred_element_type=F32) for (s, _), a in zip(pairs, weights)]
        cmax = [None] * sub
        for (s, h), pv, tot in zip(pairs, pvs, totals):
            if first:
                acc_ref[s, h] = pv
                c = jnp.broadcast_to(tot, (tq, LANES))
            else:
                c = c_ref[s, h]
                acc_ref[s, h] += jnp.exp(c) * pv
                c = c + tot
            c_ref[s, h] = c
            m = jnp.max(c)
            cmax[s] = m if cmax[s] is None else jnp.maximum(cmax[s], m)
        return cmax

    def unfinished(carry):
        b = carry[0]
        need = [(qe - b * tk > 0) & (cm > SB_SKIP_LOG) for qe, cm in zip(q_ends, carry[1:])]
        return functools.reduce(jnp.logical_or, need)

    lax.while_loop(unfinished, lambda carry: (carry[0] + 1, *walk(carry[0], False)),
                   (jnp.int32(1), *walk(0, True)))
    for s in tiles:
        o_ref[s * tq:(s + 1) * tq, :] = jnp.where(head_a, acc_ref[s, 0], acc_ref[s, 1]).astype(o_ref.dtype)


def _stick_breaking(qkv, batch, seq):
    pairs = SB_WIDTH // LANES
    tq, tk, sub = SB_TQ, SB_TK, SB_SUB
    rows = tq * sub
    later = (jnp.arange(tk)[:, None] > jnp.arange(tk)[None, :]).astype(BF16)
    return pl.pallas_call(
        functools.partial(_sb_kernel, tq=tq, tk=tk, sub=sub),
        grid=(batch, pairs, seq // rows),
        in_specs=[pl.BlockSpec((None, rows, LANES), lambda b, p, i: (b, i, p)),
                  pl.BlockSpec((None, seq, LANES), lambda b, p, i: (b, 0, pairs + p)),
                  pl.BlockSpec((None, seq, LANES), lambda b, p, i: (b, 0, 2 * pairs + p)),
                  pl.BlockSpec((tk, tk), lambda b, p, i: (0, 0))],
        out_specs=pl.BlockSpec((None, rows, LANES), lambda b, p, i: (b, i, p)),
        out_shape=jax.ShapeDtypeStruct((batch, seq, SB_WIDTH), BF16),
        scratch_shapes=[pltpu.VMEM((sub, 2, tq, LANES), F32), pltpu.VMEM((sub, 2, tq, LANES), F32)],
        compiler_params=_params("parallel", "parallel", "parallel"),
        name="stick_breaking",
    )(qkv, qkv, qkv, later)


CAT = DN_HEADS * CHUNK


def _split3(x):
    hi = x.astype(BF16)
    r = x - hi.astype(F32)
    mid = r.astype(BF16)
    return hi, mid, (r - mid.astype(F32)).astype(BF16)


def _block_diag(x, mask):
    return jnp.concatenate([x] * DN_HEADS, axis=0) * mask


def _cat_matmul(l, r, mask):
    return jnp.dot(l.astype(BF16), _block_diag(r.astype(BF16), mask), preferred_element_type=F32)


def _by_head(head, vals):
    return jnp.where(head == 0, vals[0], jnp.where(head == 1, vals[1], jnp.where(head == 2, vals[2], vals[3])))


def _gdn_constants():
    i = jnp.arange(CHUNK)
    tri = (i[:, None] >= i[None, :]).astype(BF16)
    row_head = jnp.arange(CAT)[:, None] // CHUNK
    wide_head = jnp.arange(DN_WIDTH) // DN_HEAD_DIM
    return (jnp.concatenate([tri] * 3, axis=1),
            (row_head == jnp.arange(CAT)[None, :] // CHUNK).astype(BF16),
            (row_head == wide_head[None, :]).astype(BF16),
            (row_head == jnp.concatenate([wide_head, wide_head])[None, :]).astype(BF16))


def _gdn_kernel(qkv_ref, z_ref, s_ref, alog_ref, dt_ref, og_ref, tri3_ref, bd_ref, bdk_ref,
                bdr_ref, o_ref, state_ref, sbd_ref, *, ts):
    @pl.when(pl.program_id(1) == 0)
    def _():
        state_ref[...] = jnp.zeros_like(state_ref)
        sbd_ref[...] = jnp.zeros_like(sbd_ref)

    rows = lax.broadcasted_iota(jnp.int32, (CHUNK, CAT), 0)
    lane = lax.broadcasted_iota(jnp.int32, (CHUNK, CAT), 1)
    cols = lane % CHUNK
    head = lane // CHUNK
    incl = rows >= cols
    strict = rows > cols
    upper = rows <= cols
    pair = rows // 2 == cols // 2
    lower_blocks = []
    blk = 2
    while blk < CHUNK:
        lower_blocks.append((rows // (2 * blk) == cols // (2 * blk)) & (rows // blk != cols // blk))
        blk *= 2
    neg_a = -jnp.exp(alog_ref[...])
    dt_bias = dt_ref[...]
    onorm_g = og_ref[...]
    heads = range(DN_HEADS)

    def operands(r0, beta_all, gcol):
        qs, ks, q_decs, k_decs, rhs_v, rhs_k, lasts = [], [], [], [], [], [], []
        for h in heads:
            lo, hi = h * DN_HEAD_DIM, (h + 1) * DN_HEAD_DIM
            q = qkv_ref[pl.ds(r0, CHUNK), lo:hi].astype(F32)
            k = qkv_ref[pl.ds(r0, CHUNK), DN_WIDTH + lo:DN_WIDTH + hi].astype(F32)
            v = qkv_ref[pl.ds(r0, CHUNK), 2 * DN_WIDTH + lo:2 * DN_WIDTH + hi].astype(F32)
            q = q * lax.rsqrt(jnp.sum(q * q, axis=-1, keepdims=True) + NORM_EPS) * (DN_HEAD_DIM ** -0.5)
            k = k * lax.rsqrt(jnp.sum(k * k, axis=-1, keepdims=True) + NORM_EPS)
            beta = beta_all[:, h:h + 1]
            gc = gcol[:, h * CHUNK:h * CHUNK + 1]
            glast = gc[CHUNK - 1:CHUNK, :]
            eg = jnp.exp(gc)
            qs.append(q)
            ks.append(k)
            q_decs.append(q * eg)
            k_decs.append((k * jnp.exp(glast - gc)).astype(BF16))
            rhs_v.append(beta * v)
            rhs_k.append(beta * eg * k)
            lasts.append(jnp.exp(glast))
        return (jnp.concatenate(ks, axis=1).astype(BF16), jnp.concatenate(qs, axis=1).astype(BF16),
                jnp.concatenate(rhs_v + rhs_k, axis=1), jnp.concatenate(q_decs, axis=1).astype(BF16),
                k_decs, lasts)

    def group_body(gi, _):
        group = range(GDN_GROUP)
        r0s = [pl.multiple_of((gi * GDN_GROUP + j) * CHUNK, CHUNK) for j in group]
        sms = [s_ref[pl.ds(r0, CHUNK), :] for r0 in r0s]
        beta_alls = [_sigmoid(sm) for sm in sms]
        g_alls = [neg_a * _softplus(sm + dt_bias) for sm in sms]
        beta_cats = [_by_head(head, [b[:, h:h + 1] for h in heads]) for b in beta_alls]
        g_cats = [_by_head(head, [g[:, DN_HEADS + h:DN_HEADS + h + 1] for h in heads]) for g in g_alls]
        gcols = [jnp.dot(tri3_ref[...], jnp.concatenate(_split3(g), axis=0), preferred_element_type=F32)
                 for g in g_cats]
        grows = [jnp.sum(jnp.where(upper, g, 0.0), axis=0, keepdims=True) for g in g_cats]
        gammas = [jnp.exp(jnp.where(incl, gc - gr, -jnp.inf)) for gc, gr in zip(gcols, grows)]
        ops = [operands(r0, b, gc) for r0, b, gc in zip(r0s, beta_alls, gcols)]
        bdk = bdk_ref[...]
        kqs = [lax.dot_general(jnp.concatenate([kc, qc], axis=0), _block_diag(kc, bdk), NT_DIMS,
                               preferred_element_type=F32) for kc, qc, _, _, _, _ in ops]
        qks = [(kq[CHUNK:] * gm).astype(BF16) for kq, gm in zip(kqs, gammas)]
        a_lows = [jnp.where(strict, b * kq[:CHUNK] * gm, 0.0) for b, kq, gm in zip(beta_cats, kqs, gammas)]
        bd = bd_ref[...]
        ys = [-jnp.where(pair, a, 0.0) for a in a_lows]
        for lower in lower_blocks:
            lows = [jnp.where(lower, a, 0.0) for a in a_lows]
            ts_ = [low + _cat_matmul(low, y, bd) for low, y in zip(lows, ys)]
            ys = [y - t - _cat_matmul(y, t, bd) for y, t in zip(ys, ts_)]
        bdr = bdr_ref[...]
        uws = [op[2] + _cat_matmul(y, op[2], bdr) for y, op in zip(ys, ops)]
        for j in group:
            _, _, _, q_dec, k_decs, lasts = ops[j]
            u, w = uws[j][:, :DN_WIDTH], uws[j][:, DN_WIDTH:]
            ws = jnp.dot(jnp.concatenate([w.astype(BF16), q_dec], axis=0), sbd_ref[...],
                         preferred_element_type=F32)
            vb = (u - ws[:CHUNK]).astype(BF16)
            o = ws[CHUNK:] + jnp.dot(qks[j], _block_diag(vb, bdk), preferred_element_type=F32)
            for h in heads:
                lo, hi = h * DN_HEAD_DIM, (h + 1) * DN_HEAD_DIM
                new = lasts[h] * state_ref[h] + lax.dot_general(k_decs[h], vb[:, lo:hi], TN_DIMS,
                                                                preferred_element_type=F32)
                state_ref[h] = new
                sbd_ref[lo:hi, lo:hi] = new.astype(BF16)
                oh = o[:, lo:hi]
                oh = oh * lax.rsqrt(jnp.mean(oh * oh, axis=-1, keepdims=True) + NORM_EPS) * onorm_g
                zz = z_ref[pl.ds(r0s[j], CHUNK), lo:hi].astype(F32)
                o_ref[pl.ds(r0s[j], CHUNK), lo:hi] = (oh * (zz * _sigmoid(zz))).astype(o_ref.dtype)
        return 0

    lax.fori_loop(0, ts // (CHUNK * GDN_GROUP), group_body, 0)


def _gated_deltanet(pc, pz, ps, alog_row, dt_row, onorm_row, batch, seq):
    ts = GDN_TS
    consts = _gdn_constants()
    const = lambda a: pl.BlockSpec(a.shape, lambda b, s: (0, 0))
    tile = lambda width: pl.BlockSpec((None, ts, width), lambda b, s: (b, s, 0))
    return pl.pallas_call(
        functools.partial(_gdn_kernel, ts=ts),
        grid=(batch, seq // ts),
        in_specs=[tile(3 * DN_WIDTH), tile(DN_WIDTH), tile(LANES), const(alog_row), const(dt_row),
                  const(onorm_row)] + [const(a) for a in consts],
        out_specs=tile(DN_WIDTH),
        out_shape=jax.ShapeDtypeStruct((batch, seq, DN_WIDTH), BF16),
        scratch_shapes=[pltpu.VMEM((DN_HEADS, DN_HEAD_DIM, DN_HEAD_DIM), F32),
                        pltpu.VMEM((DN_WIDTH, DN_WIDTH), BF16)],
        compiler_params=_params("parallel", "arbitrary"),
        name="gated_deltanet",
    )(pc, pz, ps, alog_row, dt_row, onorm_row, *consts)


def _top2_of4(vals):
    m1 = jnp.maximum(jnp.maximum(vals[0], vals[1]), jnp.maximum(vals[2], vals[3]))
    i1 = jnp.where(vals[0] == m1, 0, jnp.where(vals[1] == m1, 1, jnp.where(vals[2] == m1, 2, 3)))
    rest = [jnp.where(i1 == r, -jnp.inf, vals[r]) for r in range(4)]
    m2 = jnp.maximum(jnp.maximum(rest[0], rest[1]), jnp.maximum(rest[2], rest[3]))
    i2 = jnp.where(rest[0] == m2, 0, jnp.where(rest[1] == m2, 1, jnp.where(rest[2] == m2, 2, 3)))
    return m1, i1, m2, i2


def _pick4(idx, vals):
    return jnp.where(idx == 0, vals[0], jnp.where(idx == 1, vals[1], jnp.where(idx == 2, vals[2], vals[3])))


def _merge_kernel(oa_ref, ob_ref, ga_ref, gb_ref, x_ref, wa_ref, wb_ref, wo_ref, g_ref, b_ref,
                  wr_ref, rb_ref, h_ref, ri_ref, cnt_ref, carry_ref, *, tm, parts, alpha):
    t = pl.program_id(0)

    @pl.when(t == 0)
    def _():
        carry_ref[...] = jnp.zeros_like(carry_ref)

    pm = tm // parts
    subs = [slice(p * pm, (p + 1) * pm) for p in range(parts)]
    each = lambda f, *lists: [f(*args) for args in zip(*lists)]
    ya = [jnp.dot(oa_ref[r, :], wa_ref[...], preferred_element_type=F32) for r in subs]
    yb = [jnp.dot(ob_ref[r, :], wb_ref[...], preferred_element_type=F32) for r in subs]
    merged = [(_sigmoid(ga_ref[r, :].astype(F32)) * a + _sigmoid(gb_ref[r, :].astype(F32)) * b).astype(BF16)
              for r, a, b in zip(subs, ya, yb)]
    mix = [jnp.dot(m, wo_ref[...], preferred_element_type=F32) for m in merged]
    hs = [_layer_norm(alpha * x_ref[r, :] + mx, g_ref[...], b_ref[...]) for r, mx in zip(subs, mix)]
    for r, h in zip(subs, hs):
        h_ref[r, :D_MODEL] = h

    logits = [jnp.dot(h.astype(BF16), wr_ref[...], preferred_element_type=F32) for h in hs]
    lts = [lg.T[:N_EXPERTS, :] for lg in logits]
    exs = [jnp.exp(lt - jnp.max(lt, axis=0, keepdims=True)) for lt in lts]
    probs = [ex / jnp.sum(ex, axis=0, keepdims=True) for ex in exs]
    sels = [p + rb_ref[...] for p in probs]
    sel_rows = [[sel[e:e + 1, :] for e in range(N_EXPERTS)] for sel in sels]
    prob_rows = [[p[e:e + 1, :] for e in range(N_EXPERTS)] for p in probs]
    groups = range(N_GROUPS)
    local = range(EXPERTS_PER_GROUP)
    scores = [[sum(_top2_of4(rows[g * EXPERTS_PER_GROUP:(g + 1) * EXPERTS_PER_GROUP])[0:3:2]) for g in groups]
              for rows in sel_rows]
    best = [jnp.maximum(jnp.maximum(sc[0], sc[1]), jnp.maximum(sc[2], sc[3])) for sc in scores]
    grps = [jnp.where(sc[0] == bs, 0, jnp.where(sc[1] == bs, 1, jnp.where(sc[2] == bs, 2, 3)))
            for sc, bs in zip(scores, best)]
    in_grp = [[_pick4(grp, [rows[g * EXPERTS_PER_GROUP + r] for g in groups]) for r in local]
              for grp, rows in zip(grps, sel_rows)]
    p_grp = [[_pick4(grp, [rows[g * EXPERTS_PER_GROUP + r] for g in groups]) for r in local]
             for grp, rows in zip(grps, prob_rows)]
    tops = [_top2_of4(ig) for ig in in_grp]
    l0s = [tp[1] for tp in tops]
    l1s = [tp[3] for tp in tops]
    p0s = each(_pick4, l0s, p_grp)
    p1s = each(_pick4, l1s, p_grp)
    w0s = [p0 / (p0 + p1) for p0, p1 in zip(p0s, p1s)]
    w1s = [p1 / (p0 + p1) for p0, p1 in zip(p0s, p1s)]
    r128 = lax.broadcasted_iota(jnp.int32, (LANES, pm), 0)
    for r, l0, l1, w0, w1 in zip(subs, l0s, l1s, w0s, w1s):
        local_w = jnp.zeros((LANES, pm), F32)
        for e in local:
            local_w = jnp.where(r128 == e, jnp.where(l0 == e, w0, jnp.where(l1 == e, w1, 0.0)), local_w)
        h_ref[r, D_MODEL:] = local_w.T

    firsts = each(jnp.minimum, l0s, l1s)
    seconds = each(jnp.maximum, l0s, l1s)
    pair_ids = [jnp.where(f == 0, sd - 1, jnp.where(f == 1, sd + 1, PAIRS_PER_GROUP - 1))
                for f, sd in zip(firsts, seconds)]
    clss = [grp * PAIRS_PER_GROUP + pid for grp, pid in zip(grps, pair_ids)]
    cidx = lax.broadcasted_iota(jnp.int32, (CLASS_ROWS, pm), 0)
    hits = [cidx == cls for cls in clss]
    onehots = [jnp.where(hit, 1.0, 0.0) for hit in hits]
    trow = lax.broadcasted_iota(jnp.int32, (pm, pm), 0)
    tcol = lax.broadcasted_iota(jnp.int32, (pm, pm), 1)
    earlier = jnp.where(trow < tcol, 1.0, 0.0).astype(BF16)
    inside = [jnp.dot(oh.astype(BF16), earlier, preferred_element_type=F32) for oh in onehots]
    totals = [jnp.sum(oh, axis=1, keepdims=True) for oh in onehots]
    carry = carry_ref[...]
    r8 = lax.broadcasted_iota(jnp.int32, (8, pm), 0)
    for r, cls, hit, ins, tot in zip(subs, clss, hits, inside, totals):
        rank = jnp.sum(jnp.where(hit, ins + carry, 0.0), axis=0, keepdims=True).astype(jnp.int32)
        ri_ref[:, r] = jnp.where(r8 == 0, cls, jnp.where(r8 == 1, rank, 0))
        carry = carry + tot
    carry_ref[...] = carry
    cnt_ref[...] = jnp.broadcast_to(carry, cnt_ref.shape).astype(jnp.int32)


def _merge(o_a, o_b, gates, x, w_br_a, w_br_b, w_o, ln_g, ln_b, w_router, router_bias, alpha):
    n = x.shape[0]
    tm = MERGE_TM
    full = lambda shape: pl.BlockSpec(shape, lambda t: (0,) * len(shape))
    return pl.pallas_call(
        functools.partial(_merge_kernel, tm=tm, parts=MERGE_PARTS, alpha=alpha),
        grid=(n // tm,),
        in_specs=[pl.BlockSpec((tm, SB_WIDTH), lambda t: (t, 0)),
                  pl.BlockSpec((tm, DN_WIDTH), lambda t: (t, 0)),
                  pl.BlockSpec((tm, D_MODEL), lambda t: (t, 0)),
                  pl.BlockSpec((tm, D_MODEL), lambda t: (t, 1)),
                  pl.BlockSpec((tm, D_MODEL), lambda t: (t, 0)),
                  full((SB_WIDTH, D_MODEL)), full((DN_WIDTH, D_MODEL)), full((D_MODEL, D_MODEL)),
                  full((1, D_MODEL)), full((1, D_MODEL)),
                  full((D_MODEL, LANES)), full((N_EXPERTS, 1))],
        out_specs=[pl.BlockSpec((tm, H_EXT), lambda t: (t, 0)),
                   pl.BlockSpec((8, tm), lambda t: (0, t)),
                   pl.BlockSpec((CLASS_ROWS, LANES), lambda t: (0, 0))],
        out_shape=[jax.ShapeDtypeStruct((n, H_EXT), F32),
                   jax.ShapeDtypeStruct((8, n), jnp.int32),
                   jax.ShapeDtypeStruct((CLASS_ROWS, LANES), jnp.int32)],
        scratch_shapes=[pltpu.VMEM((CLASS_ROWS, 1), F32)],
        compiler_params=_params("arbitrary"),
        name="merge_ln_route",
    )(o_a, o_b, gates, gates, x, w_br_a, w_br_b, w_o, ln_g, ln_b, w_router, router_bias)


def _group_ffn_kernel(blk_grp_ref, src_ref, dst_ref, used_ref, n_used_ref, h_hbm, wg_ref, wu_ref, wd_ref, f_hbm,
                      xbuf, obuf, gsem, ssem, *, rb, n_blk):
    del blk_grp_ref
    i = pl.program_id(0)
    slot = i % 2
    other = 1 - slot

    def gather_row(block, r, s):
        return pltpu.make_async_copy(h_hbm.at[pl.ds(src_ref[block * rb + r], 1), :],
                                     xbuf.at[s, pl.ds(r, 1), :], gsem.at[s])

    def scatter_row(block, r, s):
        return pltpu.make_async_copy(obuf.at[s, pl.ds(r, 1), :],
                                     f_hbm.at[pl.ds(dst_ref[(block + 2) * rb + r], 1), :], ssem.at[s])

    @pl.when(i == 0)
    def _():
        obuf[...] = jnp.zeros_like(obuf)

        def issue(r, _):
            gather_row(0, r, 0).start()
            scatter_row(-2, r, 1).start()
            return 0

        lax.fori_loop(0, rb, issue, 0)
        pltpu.make_async_copy(obuf.at[1], f_hbm.at[pl.ds(0, rb), :], ssem.at[1]).wait()

    pltpu.make_async_copy(h_hbm.at[pl.ds(0, rb), :], xbuf.at[slot], gsem.at[slot]).wait()

    @pl.when(i > 0)
    def _():
        pltpu.make_async_copy(obuf.at[slot], f_hbm.at[pl.ds(0, rb), :], ssem.at[slot]).wait()

    @pl.when(i < n_blk)
    def _():
        x = xbuf[slot, :, :D_MODEL].astype(BF16)
        w_cols = [xbuf[slot, :, D_MODEL + e:D_MODEL + e + 1] for e in range(EXPERTS_PER_GROUP)]

        def apply(e):
            gate = jnp.dot(x, wg_ref[e], preferred_element_type=F32)
            up = jnp.dot(x, wu_ref[e], preferred_element_type=F32)
            hid = (gate * _sigmoid(gate) * up).astype(BF16)
            return w_cols[e] * jnp.dot(hid, wd_ref[e], preferred_element_type=F32)

        part = rb // 2
        for k in range(EXPERTS_PER_GROUP):
            for e in range(k, min(k + 3, EXPERTS_PER_GROUP)):
                @pl.when((used_ref[i * EXPERTS_PER_GROUP + k] == e) & (n_used_ref[i] > k))
                def _():
                    if k < 2:
                        for r in range(k * part, (k + 1) * part):
                            gather_row(i + 1, r, other).start(priority=0)
                            scatter_row(i - 1, r, other).start(priority=1)
                    if k == 0:
                        obuf[slot] = apply(e)
                    else:
                        obuf[slot] += apply(e)

    @pl.when(i == n_blk)
    def _():
        def issue(r, _):
            scatter_row(n_blk - 1, r, other).start()
            return 0

        lax.fori_loop(0, rb, issue, 0)
        pltpu.make_async_copy(obuf.at[other], f_hbm.at[pl.ds(0, rb), :], ssem.at[other]).wait()


def _group_ffn(h_ext, blk_grp, row_src, row_dst, used, n_used, w_gate, w_up, w_down, n_tok):
    rb = ROUTE_BLOCK
    n_blk = row_src.shape[0] // rb - 1
    wspec = lambda a, b: pl.BlockSpec((EXPERTS_PER_GROUP, a, b), lambda i, bg, rs, rd, us, nu: (bg[i], 0, 0))
    return pl.pallas_call(
        functools.partial(_group_ffn_kernel, rb=rb, n_blk=n_blk),
        grid_spec=pltpu.PrefetchScalarGridSpec(
            num_scalar_prefetch=5,
            grid=(n_blk + 1,),
            in_specs=[pl.BlockSpec(memory_space=pl.ANY),
                      wspec(D_MODEL, D_EXPERT), wspec(D_MODEL, D_EXPERT), wspec(D_EXPERT, D_MODEL)],
            out_specs=pl.BlockSpec(memory_space=pl.ANY),
            scratch_shapes=[pltpu.VMEM((2, rb, H_EXT), F32), pltpu.VMEM((2, rb, D_MODEL), F32),
                            pltpu.SemaphoreType.DMA((2,)), pltpu.SemaphoreType.DMA((2,))]),
        out_shape=jax.ShapeDtypeStruct((n_tok + 2 * rb, D_MODEL), F32),
        compiler_params=_params("arbitrary"),
        name="group_ffn",
    )(blk_grp, row_src, row_dst, used, n_used, h_ext, w_gate, w_up, w_down)


def _ffn_norm_kernel(h_ref, f_ref, g_ref, b_ref, o_ref, ob_ref, *, alpha):
    out = _layer_norm(alpha * h_ref[...] + f_ref[...], g_ref[...], b_ref[...])
    o_ref[...] = out
    ob_ref[...] = out.astype(BF16)


def _ffn_norm(h_ext, ffn, ln_g, ln_b, alpha):
    n = h_ext.shape[0]
    tm = NORM_TM
    row = pl.BlockSpec((tm, D_MODEL), lambda t: (t, 0))
    vec = pl.BlockSpec((1, D_MODEL), lambda t: (0, 0))
    return pl.pallas_call(
        functools.partial(_ffn_norm_kernel, alpha=alpha),
        grid=(n // tm,),
        in_specs=[row, row, vec, vec],
        out_specs=[row, row],
        out_shape=[jax.ShapeDtypeStruct((n, D_MODEL), F32), jax.ShapeDtypeStruct((n, D_MODEL), BF16)],
        compiler_params=_params("parallel"),
        name="ffn_norm",
    )(h_ext, ffn, ln_g, ln_b)


def _route_rows(ri, cnt, n_tok):
    rb = ROUTE_BLOCK
    n_cls = N_GROUPS * PAIRS_PER_GROUP
    c_count = cnt[:n_cls, 0].reshape(N_GROUPS, PAIRS_PER_GROUP)
    g_count = jnp.sum(c_count, axis=1)
    g_padded = (g_count + rb - 1) // rb * rb
    g_end = jnp.cumsum(g_padded)
    c_start = ((g_end - g_padded)[:, None] + jnp.cumsum(c_count, axis=1) - c_count).reshape(n_cls)
    c_count = c_count.reshape(n_cls)
    dest = c_start[ri[0]] + ri[1]
    p_rows = n_tok + N_GROUPS * rb
    row_tok = jnp.full((p_rows,), -1, jnp.int32).at[dest].set(jnp.arange(n_tok, dtype=jnp.int32))
    spare = n_tok + jnp.arange(-2 * rb, p_rows, dtype=jnp.int32) % (2 * rb)
    row_src = jnp.concatenate([jnp.maximum(row_tok, 0), jnp.zeros((rb,), jnp.int32)])
    row_dst = jnp.concatenate([jnp.full((2 * rb,), -1, jnp.int32), row_tok])
    row_dst = jnp.where(row_dst < 0, spare, row_dst)
    blk_start = jnp.arange(p_rows // rb + 1, dtype=jnp.int32) * rb
    blk_grp = jnp.minimum(jnp.searchsorted(g_end, blk_start, side="right"), N_GROUPS - 1).astype(jnp.int32)
    pairs = [(a, b) for a in range(EXPERTS_PER_GROUP) for b in range(a + 1, EXPERTS_PER_GROUP)]
    member = jnp.array([[e in pairs[c % PAIRS_PER_GROUP] for e in range(EXPERTS_PER_GROUP)]
                        for c in range(n_cls)])
    overlap = ((c_start[None, :] < blk_start[:, None] + rb) & ((c_start + c_count)[None, :] > blk_start[:, None])
               & (c_count[None, :] > 0))
    need = jnp.any(overlap[:, :, None] & member[None, :, :], axis=1)
    used = jnp.argsort(~need, axis=1, stable=True).astype(jnp.int32)
    n_used = jnp.maximum(jnp.sum(need, axis=1), 2).astype(jnp.int32)
    return row_src, row_dst, blk_grp, used.reshape(-1), n_used


def _lane_row(values, offset):
    return jnp.zeros((1, LANES), F32).at[0, offset:offset + values.shape[0]].set(values.astype(F32))


def kernel(x, w_in, conv_w, a_log, dt_bias, onorm_g, w_br_a, w_br_b, w_o, ln1_g, ln1_b, w_router,
           router_bias, w_gate, w_up, w_down, ln2_g, ln2_b):
    batch, seq, d = x.shape
    depth = w_in.shape[0]
    n_tok = batch * seq
    alpha = (2 * depth) ** 0.25
    a_end = 3 * SB_WIDTH
    b_end = a_end + 4 * DN_WIDTH
    s_end = b_end + 2 * DN_HEADS
    w_a = w_in[:, :, :a_end].astype(BF16)
    w_b = w_in[:, :, a_end:b_end].astype(BF16)
    w_s = jnp.pad(w_in[:, :, b_end:s_end], ((0, 0), (0, 0), (0, LANES - 2 * DN_HEADS))).astype(BF16)
    w_g = w_in[:, :, s_end:].astype(BF16)
    w_r = jnp.pad(w_router, ((0, 0), (0, LANES - N_EXPERTS))).astype(BF16)
    r_bias = router_bias.astype(F32).reshape(N_EXPERTS, 1)
    w_br_a, w_br_b, w_o = w_br_a.astype(BF16), w_br_b.astype(BF16), w_o.astype(BF16)
    w_gate, w_up, w_down = w_gate.astype(BF16), w_up.astype(BF16), w_down.astype(BF16)

    xf = x.reshape(n_tok, d)
    xb = xf.astype(BF16)
    for l in range(depth):
        pa, pc, pz, ps, gates = _project(xb, w_a[l], w_b[l], w_s[l], w_g[l], conv_w[l].astype(F32), seq)
        pa = pa.reshape(batch, seq, a_end)
        pc = pc.reshape(batch, seq, 3 * DN_WIDTH)
        pz = pz.reshape(batch, seq, DN_WIDTH)
        ps = ps.reshape(batch, seq, LANES)
        o_a = _stick_breaking(pa, batch, seq).reshape(n_tok, SB_WIDTH)
        o_b = _gated_deltanet(pc, pz, ps, _lane_row(a_log[l], DN_HEADS), _lane_row(dt_bias[l], DN_HEADS),
                              onorm_g[l].astype(F32).reshape(1, DN_HEAD_DIM), batch, seq).reshape(n_tok, DN_WIDTH)
        h_ext, ri, cnt = _merge(o_a, o_b, gates, xf, w_br_a[l], w_br_b[l], w_o[l],
                                ln1_g[l].reshape(1, d), ln1_b[l].reshape(1, d), w_r, r_bias, alpha)
        row_src, row_dst, blk_grp, used, n_used = _route_rows(ri, cnt, n_tok)
        ffn = _group_ffn(h_ext, blk_grp, row_src, row_dst, used, n_used, w_gate[l], w_up[l], w_down[l], n_tok)
        xf, xb = _ffn_norm(h_ext, ffn, ln2_g[l].reshape(1, d), ln2_b[l].reshape(1, d), alpha)
    return xf.reshape(batch, seq, d)
```

```python
import functools

import jax
import jax.numpy as jnp
from jax import lax
from jax.experimental import pallas as pl
from jax.experimental.pallas import tpu as pltpu

D_MODEL = 1024
SB_HEADS = 8
SB_HEAD_DIM = 64
SB_WIDTH = SB_HEADS * SB_HEAD_DIM
DN_HEADS = 4
DN_HEAD_DIM = 128
DN_WIDTH = DN_HEADS * DN_HEAD_DIM
CONV_K = 4
CHUNK = 64
N_EXPERTS = 16
N_GROUPS = 4
EXPERTS_PER_GROUP = N_EXPERTS // N_GROUPS
PAIRS_PER_GROUP = EXPERTS_PER_GROUP * (EXPERTS_PER_GROUP - 1) // 2
CLASS_ROWS = 32
D_EXPERT = 512
LN_EPS = 1e-5
NORM_EPS = 1e-6

LANES = 128
H_EXT = D_MODEL + LANES
VMEM_LIMIT = 48 * 1024 * 1024

PROJ_TM = 512
SB_TQ = 128
SB_TK = 256
SB_SUB = 4
GDN_TS = 512
GDN_GROUP = 8
MERGE_TM = 512
MERGE_PARTS = 2
ROUTE_BLOCK = 256
SB_SKIP_LOG = -60.0

BF16 = jnp.bfloat16
F32 = jnp.float32
NT_DIMS = (((1,), (1,)), ((), ()))
TN_DIMS = (((0,), (0,)), ((), ()))


def _params(*semantics):
    return pltpu.CompilerParams(dimension_semantics=semantics, vmem_limit_bytes=VMEM_LIMIT)


def _sigmoid(x):
    return 1.0 / (1.0 + jnp.exp(-x))


def _softplus(x):
    return jnp.maximum(x, 0.0) + jnp.log(1.0 + jnp.exp(-jnp.abs(x)))


def _layer_norm(y, g, b):
    mu = jnp.mean(y, axis=-1, keepdims=True)
    yc = y - mu
    var = jnp.mean(yc * yc, axis=-1, keepdims=True)
    return yc * lax.rsqrt(var + LN_EPS) * g + b


def _proj_kernel(x_ref, wa_ref, wb_ref, ws_ref, wg_ref, cw_ref, oa_ref, oc_ref, oz_ref, os_ref, og_ref,
                 ubuf, *, tm, tiles_per_seq):
    i = pl.program_id(0)
    tail = 8
    qkv = 3 * DN_WIDTH

    @pl.when(i == 0)
    def _():
        ubuf[tm:tm + tail, :] = jnp.zeros((tail, qkv), F32)

    x = x_ref[...].astype(BF16)
    oa_ref[...] = jnp.dot(x, wa_ref[...], preferred_element_type=F32).astype(oa_ref.dtype)
    pb = jnp.dot(x, wb_ref[...], preferred_element_type=F32)
    oz_ref[...] = pb[:, qkv:].astype(oz_ref.dtype)
    prev = ubuf[tm:tm + tail, :]
    ubuf[0:tail, :] = jnp.where(i % tiles_per_seq == 0, jnp.zeros_like(prev), prev)
    ubuf[tail:tail + tm, :] = pb[:, :qkv]
    os_ref[...] = jnp.dot(x, ws_ref[...], preferred_element_type=F32)
    gate_cols = wg_ref.shape[1]
    pieces = 4
    step = gate_cols // pieces
    lane_tiles = qkv // LANES
    for p in range(pieces):
        for c in range(p * lane_tiles // pieces, (p + 1) * lane_tiles // pieces):
            lo, hi = c * LANES, (c + 1) * LANES
            u = ubuf[:, lo:hi]
            conv = cw_ref[CONV_K - 1:CONV_K, lo:hi] * u[tail:, :]
            for j in range(CONV_K - 1):
                conv = conv + cw_ref[j:j + 1, lo:hi] * pltpu.roll(u, CONV_K - 1 - j, axis=0)[tail:, :]
            oc_ref[:, lo:hi] = (conv * _sigmoid(conv)).astype(oc_ref.dtype)
        og_ref[:, p * step:(p + 1) * step] = jnp.dot(
            x, wg_ref[:, p * step:(p + 1) * step], preferred_element_type=F32).astype(og_ref.dtype)


def _project(x, m, w_a, w_b, w_s, w_g, conv_w, seq):
    k = x.shape[1]
    tm = PROJ_TM
    qkv = 3 * DN_WIDTH
    widths = (w_a.shape[1], qkv, w_b.shape[1] - qkv, w_s.shape[1], w_g.shape[1])
    dtypes = (BF16, BF16, BF16, F32, BF16)
    const = lambda a: pl.BlockSpec(a.shape, lambda i: (0, 0))
    return pl.pallas_call(
        functools.partial(_proj_kernel, tm=tm, tiles_per_seq=seq // tm),
        grid=(m // tm,),
        in_specs=[pl.BlockSpec((tm, k), lambda i: (i, 0)), const(w_a), const(w_b), const(w_s), const(w_g),
                  const(conv_w)],
        out_specs=[pl.BlockSpec((tm, n), lambda i: (i, 0)) for n in widths],
        out_shape=[jax.ShapeDtypeStruct((m, n), dt) for n, dt in zip(widths, dtypes)],
        scratch_shapes=[pltpu.VMEM((tm + 8, qkv), F32)],
        compiler_params=_params("arbitrary"),
        name="proj",
    )(x, w_a, w_b, w_s, w_g, conv_w)


def _sb_kernel(q_ref, k_ref, v_ref, later_ref, o_ref, acc_ref, c_ref, *, tq, tk, sub):
    i = pl.program_id(2)
    lane = lax.broadcasted_iota(jnp.int32, (1, LANES), 1)
    head_a = lane < SB_HEAD_DIM
    row1 = lax.broadcasted_iota(jnp.int32, (tq, 1), 0)
    col = lax.broadcasted_iota(jnp.int32, (tq, tk), 1)
    col_b = col.astype(BF16)
    later = later_ref[...]
    tiles = range(sub)
    pairs = [(s, h) for s in tiles for h in range(2)]
    q_ends = [(i * sub + s + 1) * tq for s in tiles]
    q_heads = []
    for s in tiles:
        q = q_ref[s * tq:(s + 1) * tq, :] * jnp.asarray(SB_HEAD_DIM ** -0.5, q_ref.dtype)
        zero = jnp.zeros_like(q)
        q_heads.append((jnp.where(head_a, q, zero), jnp.where(head_a, zero, q)))

    def walk(b, first):
        nominal = [qe - (b + 1) * tk for qe in q_ends]
        starts = [pl.multiple_of(jnp.maximum(nm, 0), tq) for nm in nominal]
        ks = [k_ref[pl.ds(st, tk), :] for st in starts]
        vs = [v_ref[pl.ds(st, tk), :] for st in starts]
        bounds = [jnp.minimum(qe - tq + row1 - st, nm + tk - st) for st, qe, nm in zip(starts, q_ends, nominal)]
        valids = [col < bd for bd in bounds]
        valids_b = [col_b < bd.astype(BF16) for bd in bounds]
        zs = [lax.dot_general(q_heads[s][h], ks[s], NT_DIMS, preferred_element_type=F32) for s, h in pairs]
        zbs = [z.astype(BF16) for z in zs]
        log_betas = [jnp.minimum(zb, 0) - jnp.log(1 + jnp.exp(-jnp.abs(zb))) for zb in zbs]
        log_keeps = [jnp.where(valids_b[s], lb - zb, 0) for (s, _), lb, zb in zip(pairs, log_betas, zbs)]
        sums = [jnp.dot(lk, later, preferred_element_type=F32) for lk in log_keeps]
        weights = [jnp.where(valids[s], jnp.exp(lb.astype(F32) + sm), 0.0).astype(BF16)
                   for (s, _), lb, sm in zip(pairs, log_betas, sums)]
        totals = [sm[:, 0:1] + lk[:, 0:1].astype(F32) for sm, lk in zip(sums, log_keeps)]
        pvs = [jnp.dot(a, vs[s], preferred_element_type=F32) for (s, _), a in zip(pairs, weights)]
        cmax = [None] * sub
        for (s, h), pv, tot in zip(pairs, pvs, totals):
            if first:
                acc_ref[s, h] = pv
                c = jnp.broadcast_to(tot, (tq, LANES))
            else:
                c = c_ref[s, h]
                acc_ref[s, h] += jnp.exp(c) * pv
                c = c + tot
            c_ref[s, h] = c
            m = jnp.max(c)
            cmax[s] = m if cmax[s] is None else jnp.maximum(cmax[s], m)
        return cmax

    def unfinished(carry):
        b = carry[0]
        need = [(qe - b * tk > 0) & (cm > SB_SKIP_LOG) for qe, cm in zip(q_ends, carry[1:])]
        return functools.reduce(jnp.logical_or, need)

    lax.while_loop(unfinished, lambda carry: (carry[0] + 1, *walk(carry[0], False)),
                   (jnp.int32(1), *walk(0, True)))
    for s in tiles:
        o_ref[s * tq:(s + 1) * tq, :] = jnp.where(head_a, acc_ref[s, 0], acc_ref[s, 1]).astype(o_ref.dtype)


def _stick_breaking(qkv, batch, seq):
    pairs = SB_WIDTH // LANES
    tq, tk, sub = SB_TQ, SB_TK, SB_SUB
    rows = tq * sub
    later = (jnp.arange(tk)[:, None] > jnp.arange(tk)[None, :]).astype(BF16)
    return pl.pallas_call(
        functools.partial(_sb_kernel, tq=tq, tk=tk, sub=sub),
        grid=(batch, pairs, seq // rows),
        in_specs=[pl.BlockSpec((None, rows, LANES), lambda b, p, i: (b, i, p)),
                  pl.BlockSpec((None, seq, LANES), lambda b, p, i: (b, 0, pairs + p)),
                  pl.BlockSpec((None, seq, LANES), lambda b, p, i: (b, 0, 2 * pairs + p)),
                  pl.BlockSpec((tk, tk), lambda b, p, i: (0, 0))],
        out_specs=pl.BlockSpec((None, rows, LANES), lambda b, p, i: (b, i, p)),
        out_shape=jax.ShapeDtypeStruct((batch, seq, SB_WIDTH), BF16),
        scratch_shapes=[pltpu.VMEM((sub, 2, tq, LANES), F32), pltpu.VMEM((sub, 2, tq, LANES), F32)],
        compiler_params=_params("parallel", "parallel", "parallel"),
        name="stick_breaking",
    )(qkv, qkv, qkv, later)


CAT = DN_HEADS * CHUNK


def _split3(x):
    hi = x.astype(BF16)
    r = x - hi.astype(F32)
    mid = r.astype(BF16)
    return hi, mid, (r - mid.astype(F32)).astype(BF16)


def _block_diag(x, mask):
    return jnp.concatenate([x] * DN_HEADS, axis=0) * mask


def _cat_matmul(l, r, mask):
    return jnp.dot(l.astype(BF16), _block_diag(r.astype(BF16), mask), preferred_element_type=F32)


def _by_head(head, vals):
    return jnp.where(head == 0, vals[0], jnp.where(head == 1, vals[1], jnp.where(head == 2, vals[2], vals[3])))


def _gdn_constants():
    i = jnp.arange(CHUNK)
    tri = (i[:, None] >= i[None, :]).astype(BF16)
    row_head = jnp.arange(CAT)[:, None] // CHUNK
    wide_head = jnp.arange(DN_WIDTH) // DN_HEAD_DIM
    return (jnp.concatenate([tri] * 3, axis=1),
            (row_head == jnp.arange(CAT)[None, :] // CHUNK).astype(BF16),
            (row_head == wide_head[None, :]).astype(BF16),
            (row_head == jnp.concatenate([wide_head, wide_head])[None, :]).astype(BF16))


def _gdn_kernel(qkv_ref, z_ref, s_ref, alog_ref, dt_ref, og_ref, tri3_ref, bd_ref, bdk_ref,
                bdr_ref, o_ref, state_ref, sbd_ref, *, ts):
    @pl.when(pl.program_id(1) == 0)
    def _():
        state_ref[...] = jnp.zeros_like(state_ref)
        sbd_ref[...] = jnp.zeros_like(sbd_ref)

    rows = lax.broadcasted_iota(jnp.int32, (CHUNK, CAT), 0)
    lane = lax.broadcasted_iota(jnp.int32, (CHUNK, CAT), 1)
    cols = lane % CHUNK
    head = lane // CHUNK
    incl = rows >= cols
    strict = rows > cols
    upper = rows <= cols
    pair = rows // 2 == cols // 2
    lower_blocks = []
    blk = 2
    while blk < CHUNK:
        lower_blocks.append((rows // (2 * blk) == cols // (2 * blk)) & (rows // blk != cols // blk))
        blk *= 2
    neg_a = -jnp.exp(alog_ref[...])
    dt_bias = dt_ref[...]
    onorm_g = og_ref[...]
    heads = range(DN_HEADS)

    def operands(r0, beta_all, gcol):
        qs, ks, q_decs, k_decs, rhs_v, rhs_k, lasts = [], [], [], [], [], [], []
        for h in heads:
            lo, hi = h * DN_HEAD_DIM, (h + 1) * DN_HEAD_DIM
            q = qkv_ref[pl.ds(r0, CHUNK), lo:hi].astype(F32)
            k = qkv_ref[pl.ds(r0, CHUNK), DN_WIDTH + lo:DN_WIDTH + hi].astype(F32)
            v = qkv_ref[pl.ds(r0, CHUNK), 2 * DN_WIDTH + lo:2 * DN_WIDTH + hi].astype(F32)
            q = q * lax.rsqrt(jnp.sum(q * q, axis=-1, keepdims=True) + NORM_EPS) * (DN_HEAD_DIM ** -0.5)
            k = k * lax.rsqrt(jnp.sum(k * k, axis=-1, keepdims=True) + NORM_EPS)
            beta = beta_all[:, h:h + 1]
            gc = gcol[:, h * CHUNK:h * CHUNK + 1]
            glast = gc[CHUNK - 1:CHUNK, :]
            eg = jnp.exp(gc)
            qs.append(q)
            ks.append(k)
            q_decs.append(q * eg)
            k_decs.append((k * jnp.exp(glast - gc)).astype(BF16))
            rhs_v.append(beta * v)
            rhs_k.append(beta * eg * k)
            lasts.append(jnp.exp(glast))
        return (jnp.concatenate(ks, axis=1).astype(BF16), jnp.concatenate(qs, axis=1).astype(BF16),
                jnp.concatenate(rhs_v + rhs_k, axis=1), jnp.concatenate(q_decs, axis=1).astype(BF16),
                k_decs, lasts)

    def group_body(gi, _):
        group = range(GDN_GROUP)
        r0s = [pl.multiple_of((gi * GDN_GROUP + j) * CHUNK, CHUNK) for j in group]
        sms = [s_ref[pl.ds(r0, CHUNK), :] for r0 in r0s]
        beta_alls = [_sigmoid(sm) for sm in sms]
        g_alls = [neg_a * _softplus(sm + dt_bias) for sm in sms]
        beta_cats = [_by_head(head, [b[:, h:h + 1] for h in heads]) for b in beta_alls]
        g_cats = [_by_head(head, [g[:, DN_HEADS + h:DN_HEADS + h + 1] for h in heads]) for g in g_alls]
        gcols = [jnp.dot(tri3_ref[...], jnp.concatenate(_split3(g), axis=0), preferred_element_type=F32)
                 for g in g_cats]
        grows = [jnp.sum(jnp.where(upper, g, 0.0), axis=0, keepdims=True) for g in g_cats]
        gammas = [jnp.exp(jnp.where(incl, gc - gr, -jnp.inf)) for gc, gr in zip(gcols, grows)]
        ops = [operands(r0, b, gc) for r0, b, gc in zip(r0s, beta_alls, gcols)]
        bdk = bdk_ref[...]
        kqs = [lax.dot_general(jnp.concatenate([kc, qc], axis=0), _block_diag(kc, bdk), NT_DIMS,
                               preferred_element_type=F32) for kc, qc, _, _, _, _ in ops]
        qks = [(kq[CHUNK:] * gm).astype(BF16) for kq, gm in zip(kqs, gammas)]
        a_lows = [jnp.where(strict, b * kq[:CHUNK] * gm, 0.0) for b, kq, gm in zip(beta_cats, kqs, gammas)]
        bd = bd_ref[...]
        ys = [-jnp.where(pair, a, 0.0) for a in a_lows]
        for lower in lower_blocks:
            lows = [jnp.where(lower, a, 0.0) for a in a_lows]
            ts_ = [low + _cat_matmul(low, y, bd) for low, y in zip(lows, ys)]
            ys = [y - t - _cat_matmul(y, t, bd) for y, t in zip(ys, ts_)]
        bdr = bdr_ref[...]
        uws = [op[2] + _cat_matmul(y, op[2], bdr) for y, op in zip(ys, ops)]
        for j in group:
            _, _, _, q_dec, k_decs, lasts = ops[j]
            u, w = uws[j][:, :DN_WIDTH], uws[j][:, DN_WIDTH:]
            ws = jnp.dot(jnp.concatenate([w.astype(BF16), q_dec], axis=0), sbd_ref[...],
                         preferred_element_type=F32)
            vb = (u - ws[:CHUNK]).astype(BF16)
            o = ws[CHUNK:] + jnp.dot(qks[j], _block_diag(vb, bdk), preferred_element_type=F32)
            for h in heads:
                lo, hi = h * DN_HEAD_DIM, (h + 1) * DN_HEAD_DIM
                new = lasts[h] * state_ref[h] + lax.dot_general(k_decs[h], vb[:, lo:hi], TN_DIMS,
                                                                preferred_element_type=F32)
                state_ref[h] = new
                sbd_ref[lo:hi, lo:hi] = new.astype(BF16)
                oh = o[:, lo:hi]
                oh = oh * lax.rsqrt(jnp.mean(oh * oh, axis=-1, keepdims=True) + NORM_EPS) * onorm_g
                zz = z_ref[pl.ds(r0s[j], CHUNK), lo:hi].astype(F32)
                o_ref[pl.ds(r0s[j], CHUNK), lo:hi] = (oh * (zz * _sigmoid(zz))).astype(o_ref.dtype)
        return 0

    lax.fori_loop(0, ts // (CHUNK * GDN_GROUP), group_body, 0)


def _gated_deltanet(pc, pz, ps, alog_row, dt_row, onorm_row, batch, seq):
    ts = GDN_TS
    consts = _gdn_constants()
    const = lambda a: pl.BlockSpec(a.shape, lambda b, s: (0, 0))
    tile = lambda width: pl.BlockSpec((None, ts, width), lambda b, s: (b, s, 0))
    return pl.pallas_call(
        functools.partial(_gdn_kernel, ts=ts),
        grid=(batch, seq // ts),
        in_specs=[tile(3 * DN_WIDTH), tile(DN_WIDTH), tile(LANES), const(alog_row), const(dt_row),
                  const(onorm_row)] + [const(a) for a in consts],
        out_specs=tile(DN_WIDTH),
        out_shape=jax.ShapeDtypeStruct((batch, seq, DN_WIDTH), BF16),
        scratch_shapes=[pltpu.VMEM((DN_HEADS, DN_HEAD_DIM, DN_HEAD_DIM), F32),
                        pltpu.VMEM((DN_WIDTH, DN_WIDTH), BF16)],
        compiler_params=_params("parallel", "arbitrary"),
        name="gated_deltanet",
    )(pc, pz, ps, alog_row, dt_row, onorm_row, *consts)


def _top2_of4(vals):
    m1 = jnp.maximum(jnp.maximum(vals[0], vals[1]), jnp.maximum(vals[2], vals[3]))
    i1 = jnp.where(vals[0] == m1, 0, jnp.where(vals[1] == m1, 1, jnp.where(vals[2] == m1, 2, 3)))
    rest = [jnp.where(i1 == r, -jnp.inf, vals[r]) for r in range(4)]
    m2 = jnp.maximum(jnp.maximum(rest[0], rest[1]), jnp.maximum(rest[2], rest[3]))
    i2 = jnp.where(rest[0] == m2, 0, jnp.where(rest[1] == m2, 1, jnp.where(rest[2] == m2, 2, 3)))
    return m1, i1, m2, i2


def _pick4(idx, vals):
    return jnp.where(idx == 0, vals[0], jnp.where(idx == 1, vals[1], jnp.where(idx == 2, vals[2], vals[3])))


def _merge_kernel(oa_ref, ob_ref, ga_ref, gb_ref, x_ref, wa_ref, wb_ref, wo_ref, g_ref, b_ref,
                  wr_ref, rb_ref, h_ref, ri_ref, cnt_ref, carry_ref, *, tm, parts, alpha):
    t = pl.program_id(0)

    @pl.when(t == 0)
    def _():
        carry_ref[...] = jnp.zeros_like(carry_ref)

    pm = tm // parts
    subs = [slice(p * pm, (p + 1) * pm) for p in range(parts)]
    each = lambda f, *lists: [f(*args) for args in zip(*lists)]
    ya = [jnp.dot(oa_ref[r, :], wa_ref[...], preferred_element_type=F32) for r in subs]
    yb = [jnp.dot(ob_ref[r, :], wb_ref[...], preferred_element_type=F32) for r in subs]
    merged = [(_sigmoid(ga_ref[r, :].astype(F32)) * a + _sigmoid(gb_ref[r, :].astype(F32)) * b).astype(BF16)
              for r, a, b in zip(subs, ya, yb)]
    mix = [jnp.dot(m, wo_ref[...], preferred_element_type=F32) for m in merged]
    hs = [_layer_norm(alpha * x_ref[r, :] + mx, g_ref[...], b_ref[...]) for r, mx in zip(subs, mix)]
    for r, h in zip(subs, hs):
        h_ref[r, :D_MODEL] = h

    logits = [jnp.dot(h.astype(BF16), wr_ref[...], preferred_element_type=F32) for h in hs]
    lts = [lg.T[:N_EXPERTS, :] for lg in logits]
    exs = [jnp.exp(lt - jnp.max(lt, axis=0, keepdims=True)) for lt in lts]
    probs = [ex / jnp.sum(ex, axis=0, keepdims=True) for ex in exs]
    sels = [p + rb_ref[...] for p in probs]
    sel_rows = [[sel[e:e + 1, :] for e in range(N_EXPERTS)] for sel in sels]
    prob_rows = [[p[e:e + 1, :] for e in range(N_EXPERTS)] for p in probs]
    groups = range(N_GROUPS)
    local = range(EXPERTS_PER_GROUP)
    scores = [[sum(_top2_of4(rows[g * EXPERTS_PER_GROUP:(g + 1) * EXPERTS_PER_GROUP])[0:3:2]) for g in groups]
              for rows in sel_rows]
    best = [jnp.maximum(jnp.maximum(sc[0], sc[1]), jnp.maximum(sc[2], sc[3])) for sc in scores]
    grps = [jnp.where(sc[0] == bs, 0, jnp.where(sc[1] == bs, 1, jnp.where(sc[2] == bs, 2, 3)))
            for sc, bs in zip(scores, best)]
    in_grp = [[_pick4(grp, [rows[g * EXPERTS_PER_GROUP + r] for g in groups]) for r in local]
              for grp, rows in zip(grps, sel_rows)]
    p_grp = [[_pick4(grp, [rows[g * EXPERTS_PER_GROUP + r] for g in groups]) for r in local]
             for grp, rows in zip(grps, prob_rows)]
    tops = [_top2_of4(ig) for ig in in_grp]
    l0s = [tp[1] for tp in tops]
    l1s = [tp[3] for tp in tops]
    p0s = each(_pick4, l0s, p_grp)
    p1s = each(_pick4, l1s, p_grp)
    w0s = [p0 / (p0 + p1) for p0, p1 in zip(p0s, p1s)]
    w1s = [p1 / (p0 + p1) for p0, p1 in zip(p0s, p1s)]
    r128 = lax.broadcasted_iota(jnp.int32, (LANES, pm), 0)
    for r, l0, l1, w0, w1 in zip(subs, l0s, l1s, w0s, w1s):
        local_w = jnp.zeros((LANES, pm), F32)
        for e in local:
            local_w = jnp.where(r128 == e, jnp.where(l0 == e, w0, jnp.where(l1 == e, w1, 0.0)), local_w)
        h_ref[r, D_MODEL:] = local_w.T

    firsts = each(jnp.minimum, l0s, l1s)
    seconds = each(jnp.maximum, l0s, l1s)
    pair_ids = [jnp.where(f == 0, sd - 1, jnp.where(f == 1, sd + 1, PAIRS_PER_GROUP - 1))
                for f, sd in zip(firsts, seconds)]
    clss = [grp * PAIRS_PER_GROUP + pid for grp, pid in zip(grps, pair_ids)]
    cidx = lax.broadcasted_iota(jnp.int32, (CLASS_ROWS, pm), 0)
    hits = [cidx == cls for cls in clss]
    onehots = [jnp.where(hit, 1.0, 0.0) for hit in hits]
    trow = lax.broadcasted_iota(jnp.int32, (pm, pm), 0)
    tcol = lax.broadcasted_iota(jnp.int32, (pm, pm), 1)
    earlier = jnp.where(trow < tcol, 1.0, 0.0).astype(BF16)
    inside = [jnp.dot(oh.astype(BF16), earlier, preferred_element_type=F32) for oh in onehots]
    totals = [jnp.sum(oh, axis=1, keepdims=True) for oh in onehots]
    carry = carry_ref[...]
    r8 = lax.broadcasted_iota(jnp.int32, (8, pm), 0)
    for r, cls, hit, ins, tot in zip(subs, clss, hits, inside, totals):
        rank = jnp.sum(jnp.where(hit, ins + carry, 0.0), axis=0, keepdims=True).astype(jnp.int32)
        ri_ref[:, r] = jnp.where(r8 == 0, cls, jnp.where(r8 == 1, rank, 0))
        carry = carry + tot
    carry_ref[...] = carry
    cnt_ref[...] = jnp.broadcast_to(carry, cnt_ref.shape).astype(jnp.int32)


def _merge(o_a, o_b, gates, x, w_br_a, w_br_b, w_o, ln_g, ln_b, w_router, router_bias, alpha):
    n = o_a.shape[0]
    tm = MERGE_TM
    full = lambda shape: pl.BlockSpec(shape, lambda t: (0,) * len(shape))
    return pl.pallas_call(
        functools.partial(_merge_kernel, tm=tm, parts=MERGE_PARTS, alpha=alpha),
        grid=(n // tm,),
        in_specs=[pl.BlockSpec((tm, SB_WIDTH), lambda t: (t, 0)),
                  pl.BlockSpec((tm, DN_WIDTH), lambda t: (t, 0)),
                  pl.BlockSpec((tm, D_MODEL), lambda t: (t, 0)),
                  pl.BlockSpec((tm, D_MODEL), lambda t: (t, 1)),
                  pl.BlockSpec((tm, D_MODEL), lambda t: (t, 0)),
                  full((SB_WIDTH, D_MODEL)), full((DN_WIDTH, D_MODEL)), full((D_MODEL, D_MODEL)),
                  full((1, D_MODEL)), full((1, D_MODEL)),
                  full((D_MODEL, LANES)), full((N_EXPERTS, 1))],
        out_specs=[pl.BlockSpec((tm, H_EXT), lambda t: (t, 0)),
                   pl.BlockSpec((8, tm), lambda t: (0, t)),
                   pl.BlockSpec((CLASS_ROWS, LANES), lambda t: (0, 0))],
        out_shape=[jax.ShapeDtypeStruct((n, H_EXT), F32),
                   jax.ShapeDtypeStruct((8, n), jnp.int32),
                   jax.ShapeDtypeStruct((CLASS_ROWS, LANES), jnp.int32)],
        scratch_shapes=[pltpu.VMEM((CLASS_ROWS, 1), F32)],
        compiler_params=_params("arbitrary"),
        name="merge_ln_route",
    )(o_a, o_b, gates, gates, x, w_br_a, w_br_b, w_o, ln_g, ln_b, w_router, router_bias)


def _group_ffn_kernel(blk_grp_ref, src_ref, dst_ref, used_ref, n_used_ref, h_hbm, wg_ref, wu_ref, wd_ref,
                      g_ref, b_ref, f_hbm, xbuf, obuf, gsem, ssem, *, rb, n_blk, alpha):
    del blk_grp_ref
    i = pl.program_id(0)
    slot = i % 2
    other = 1 - slot

    def gather_row(block, r, s):
        return pltpu.make_async_copy(h_hbm.at[pl.ds(src_ref[block * rb + r], 1), :],
                                     xbuf.at[s, pl.ds(r, 1), :], gsem.at[s])

    def scatter_row(block, r, s):
        return pltpu.make_async_copy(obuf.at[s, pl.ds(r, 1), :],
                                     f_hbm.at[pl.ds(dst_ref[(block + 2) * rb + r], 1), :], ssem.at[s])

    @pl.when(i == 0)
    def _():
        obuf[...] = jnp.zeros_like(obuf)

        def issue(r, _):
            gather_row(0, r, 0).start()
            scatter_row(-2, r, 1).start()
            return 0

        lax.fori_loop(0, rb, issue, 0)
        pltpu.make_async_copy(obuf.at[1], f_hbm.at[pl.ds(0, rb), :], ssem.at[1]).wait()

    pltpu.make_async_copy(h_hbm.at[pl.ds(0, rb), :], xbuf.at[slot], gsem.at[slot]).wait()

    @pl.when(i > 0)
    def _():
        pltpu.make_async_copy(obuf.at[slot], f_hbm.at[pl.ds(0, rb), :], ssem.at[slot]).wait()

    @pl.when(i < n_blk)
    def _():
        x = xbuf[slot, :, :D_MODEL].astype(BF16)
        w_cols = [xbuf[slot, :, D_MODEL + e:D_MODEL + e + 1] for e in range(EXPERTS_PER_GROUP)]

        def apply(e):
            gate = jnp.dot(x, wg_ref[e], preferred_element_type=F32)
            up = jnp.dot(x, wu_ref[e], preferred_element_type=F32)
            hid = (gate * _sigmoid(gate) * up).astype(BF16)
            return w_cols[e] * jnp.dot(hid, wd_ref[e], preferred_element_type=F32)

        part = rb // 2
        for k in range(EXPERTS_PER_GROUP):
            for e in range(k, min(k + 3, EXPERTS_PER_GROUP)):
                @pl.when((used_ref[i * EXPERTS_PER_GROUP + k] == e) & (n_used_ref[i] > k))
                def _():
                    if k < 2:
                        for r in range(k * part, (k + 1) * part):
                            gather_row(i + 1, r, other).start(priority=0)
                            scatter_row(i - 1, r, other).start(priority=1)
                    if k == 0:
                        obuf[slot] = apply(e)
                    else:
                        obuf[slot] += apply(e)

        obuf[slot] = _layer_norm(alpha * xbuf[slot, :, :D_MODEL] + obuf[slot], g_ref[...], b_ref[...])

    @pl.when(i == n_blk)
    def _():
        def issue(r, _):
            scatter_row(n_blk - 1, r, other).start()
            return 0

        lax.fori_loop(0, rb, issue, 0)
        pltpu.make_async_copy(obuf.at[other], f_hbm.at[pl.ds(0, rb), :], ssem.at[other]).wait()


def _group_ffn(h_ext, blk_grp, row_src, row_dst, used, n_used, w_gate, w_up, w_down, ln_g, ln_b, alpha, n_tok):
    rb = ROUTE_BLOCK
    n_blk = row_src.shape[0] // rb - 1
    wspec = lambda a, b: pl.BlockSpec((EXPERTS_PER_GROUP, a, b), lambda i, bg, rs, rd, us, nu: (bg[i], 0, 0))
    vec = pl.BlockSpec((1, D_MODEL), lambda i, bg, rs, rd, us, nu: (0, 0))
    return pl.pallas_call(
        functools.partial(_group_ffn_kernel, rb=rb, n_blk=n_blk, alpha=alpha),
        grid_spec=pltpu.PrefetchScalarGridSpec(
            num_scalar_prefetch=5,
            grid=(n_blk + 1,),
            in_specs=[pl.BlockSpec(memory_space=pl.ANY),
                      wspec(D_MODEL, D_EXPERT), wspec(D_MODEL, D_EXPERT), wspec(D_EXPERT, D_MODEL),
                      vec, vec],
            out_specs=pl.BlockSpec(memory_space=pl.ANY),
            scratch_shapes=[pltpu.VMEM((2, rb, H_EXT), F32), pltpu.VMEM((2, rb, D_MODEL), F32),
                            pltpu.SemaphoreType.DMA((2,)), pltpu.SemaphoreType.DMA((2,))]),
        out_shape=jax.ShapeDtypeStruct((n_tok + 2 * rb, D_MODEL), F32),
        compiler_params=_params("arbitrary"),
        name="group_ffn",
    )(blk_grp, row_src, row_dst, used, n_used, h_ext, w_gate, w_up, w_down, ln_g, ln_b)


def _route_rows(ri, cnt, n_tok):
    rb = ROUTE_BLOCK
    n_cls = N_GROUPS * PAIRS_PER_GROUP
    c_count = cnt[:n_cls, 0].reshape(N_GROUPS, PAIRS_PER_GROUP)
    g_count = jnp.sum(c_count, axis=1)
    g_padded = (g_count + rb - 1) // rb * rb
    g_end = jnp.cumsum(g_padded)
    c_start = ((g_end - g_padded)[:, None] + jnp.cumsum(c_count, axis=1) - c_count).reshape(n_cls)
    c_count = c_count.reshape(n_cls)
    dest = c_start[ri[0]] + ri[1]
    p_rows = n_tok + N_GROUPS * rb
    row_tok = jnp.full((p_rows,), -1, jnp.int32).at[dest].set(jnp.arange(n_tok, dtype=jnp.int32))
    spare = n_tok + jnp.arange(-2 * rb, p_rows, dtype=jnp.int32) % (2 * rb)
    row_src = jnp.concatenate([jnp.maximum(row_tok, 0), jnp.zeros((rb,), jnp.int32)])
    row_dst = jnp.concatenate([jnp.full((2 * rb,), -1, jnp.int32), row_tok])
    row_dst = jnp.where(row_dst < 0, spare, row_dst)
    blk_start = jnp.arange(p_rows // rb + 1, dtype=jnp.int32) * rb
    blk_grp = jnp.minimum(jnp.searchsorted(g_end, blk_start, side="right"), N_GROUPS - 1).astype(jnp.int32)
    pairs = [(a, b) for a in range(EXPERTS_PER_GROUP) for b in range(a + 1, EXPERTS_PER_GROUP)]
    member = jnp.array([[e in pairs[c % PAIRS_PER_GROUP] for e in range(EXPERTS_PER_GROUP)]
                        for c in range(n_cls)])
    overlap = ((c_start[None, :] < blk_start[:, None] + rb) & ((c_start + c_count)[None, :] > blk_start[:, None])
               & (c_count[None, :] > 0))
    need = jnp.any(overlap[:, :, None] & member[None, :, :], axis=1)
    used = jnp.argsort(~need, axis=1, stable=True).astype(jnp.int32)
    n_used = jnp.maximum(jnp.sum(need, axis=1), 2).astype(jnp.int32)
    return row_src, row_dst, blk_grp, used.reshape(-1), n_used


def _lane_row(values, offset):
    return jnp.zeros((1, LANES), F32).at[0, offset:offset + values.shape[0]].set(values.astype(F32))


def kernel(x, w_in, conv_w, a_log, dt_bias, onorm_g, w_br_a, w_br_b, w_o, ln1_g, ln1_b, w_router,
           router_bias, w_gate, w_up, w_down, ln2_g, ln2_b):
    batch, seq, d = x.shape
    depth = w_in.shape[0]
    n_tok = batch * seq
    alpha = (2 * depth) ** 0.25
    a_end = 3 * SB_WIDTH
    b_end = a_end + 4 * DN_WIDTH
    s_end = b_end + 2 * DN_HEADS
    w_a = w_in[:, :, :a_end].astype(BF16)
    w_b = w_in[:, :, a_end:b_end].astype(BF16)
    w_s = jnp.pad(w_in[:, :, b_end:s_end], ((0, 0), (0, 0), (0, LANES - 2 * DN_HEADS))).astype(BF16)
    w_g = w_in[:, :, s_end:].astype(BF16)
    w_r = jnp.pad(w_router, ((0, 0), (0, LANES - N_EXPERTS))).astype(BF16)
    r_bias = router_bias.astype(F32).reshape(N_EXPERTS, 1)
    w_br_a, w_br_b, w_o = w_br_a.astype(BF16), w_br_b.astype(BF16), w_o.astype(BF16)
    w_gate, w_up, w_down = w_gate.astype(BF16), w_up.astype(BF16), w_down.astype(BF16)

    xf = x.reshape(n_tok, d)
    for l in range(depth):
        pa, pc, pz, ps, gates = _project(xf, n_tok, w_a[l], w_b[l], w_s[l], w_g[l], conv_w[l].astype(F32), seq)
        pa = pa.reshape(batch, seq, a_end)
        pc = pc.reshape(batch, seq, 3 * DN_WIDTH)
        pz = pz.reshape(batch, seq, DN_WIDTH)
        ps = ps.reshape(batch, seq, LANES)
        o_a = _stick_breaking(pa, batch, seq).reshape(n_tok, SB_WIDTH)
        o_b = _gated_deltanet(pc, pz, ps, _lane_row(a_log[l], DN_HEADS), _lane_row(dt_bias[l], DN_HEADS),
                              onorm_g[l].astype(F32).reshape(1, DN_HEAD_DIM), batch, seq).reshape(n_tok, DN_WIDTH)
        h_ext, ri, cnt = _merge(o_a, o_b, gates, xf, w_br_a[l], w_br_b[l], w_o[l],
                                ln1_g[l].reshape(1, d), ln1_b[l].reshape(1, d), w_r, r_bias, alpha)
        row_src, row_dst, blk_grp, used, n_used = _route_rows(ri, cnt, n_tok)
        xf = _group_ffn(h_ext, blk_grp, row_src, row_dst, used, n_used, w_gate[l], w_up[l], w_down[l],
                        ln2_g[l].reshape(1, d), ln2_b[l].reshape(1, d), alpha, n_tok)
    return xf[:n_tok].reshape(batch, seq, d)
```

```python
import functools

import jax
import jax.numpy as jnp
from jax import lax
from jax.experimental import pallas as pl
from jax.experimental.pallas import tpu as pltpu

D_MODEL = 1024
SB_HEADS = 8
SB_HEAD_DIM = 64
SB_WIDTH = SB_HEADS * SB_HEAD_DIM
DN_HEADS = 4
DN_HEAD_DIM = 128
DN_WIDTH = DN_HEADS * DN_HEAD_DIM
CONV_K = 4
CHUNK = 64
N_EXPERTS = 16
N_GROUPS = 4
EXPERTS_PER_GROUP = N_EXPERTS // N_GROUPS
PAIRS_PER_GROUP = EXPERTS_PER_GROUP * (EXPERTS_PER_GROUP - 1) // 2
CLASS_ROWS = 32
D_EXPERT = 512
LN_EPS = 1e-5
NORM_EPS = 1e-6

LANES = 128
H_EXT = D_MODEL + LANES
VMEM_LIMIT = 48 * 1024 * 1024

PROJ_TM = 512
SB_TQ = 128
SB_TK = 256
SB_SUB = 8
GDN_TS = 512
GDN_GROUP = 8
MERGE_TM = 1024
MERGE_PARTS = 4
ROUTE_BLOCK = 256
SB_SKIP_LOG = -60.0

BF16 = jnp.bfloat16
F32 = jnp.float32
NT_DIMS = (((1,), (1,)), ((), ()))
TN_DIMS = (((0,), (0,)), ((), ()))


def _params(*semantics):
    return pltpu.CompilerParams(dimension_semantics=semantics, vmem_limit_bytes=VMEM_LIMIT)


def _sigmoid(x):
    return 1.0 / (1.0 + jnp.exp(-x))


def _softplus(x):
    return jnp.maximum(x, 0.0) + jnp.log(1.0 + jnp.exp(-jnp.abs(x)))


def _layer_norm(y, g, b):
    mu = jnp.mean(y, axis=-1, keepdims=True)
    yc = y - mu
    var = jnp.mean(yc * yc, axis=-1, keepdims=True)
    return yc * lax.rsqrt(var + LN_EPS) * g + b


def _proj_kernel(x_ref, wa_ref, wb_ref, ws_ref, wg_ref, cw_ref, oa_ref, oc_ref, oz_ref, os_ref, og_ref,
                 ubuf, *, tm, tiles_per_seq):
    i = pl.program_id(0)
    tail = 8
    qkv = 3 * DN_WIDTH

    @pl.when(i == 0)
    def _():
        ubuf[tm:tm + tail, :] = jnp.zeros((tail, qkv), F32)

    x = x_ref[...].astype(BF16)
    oa_ref[...] = jnp.dot(x, wa_ref[...], preferred_element_type=F32).astype(oa_ref.dtype)
    pb = jnp.dot(x, wb_ref[...], preferred_element_type=F32)
    oz_ref[...] = pb[:, qkv:].astype(oz_ref.dtype)
    prev = ubuf[tm:tm + tail, :]
    ubuf[0:tail, :] = jnp.where(i % tiles_per_seq == 0, jnp.zeros_like(prev), prev)
    ubuf[tail:tail + tm, :] = pb[:, :qkv]
    os_ref[...] = jnp.dot(x, ws_ref[...], preferred_element_type=F32)
    gate_cols = wg_ref.shape[1]
    pieces = 4
    step = gate_cols // pieces
    lane_tiles = qkv // LANES
    for p in range(pieces):
        for c in range(p * lane_tiles // pieces, (p + 1) * lane_tiles // pieces):
            lo, hi = c * LANES, (c + 1) * LANES
            u = ubuf[:, lo:hi]
            conv = cw_ref[CONV_K - 1:CONV_K, lo:hi] * u[tail:, :]
            for j in range(CONV_K - 1):
                conv = conv + cw_ref[j:j + 1, lo:hi] * pltpu.roll(u, CONV_K - 1 - j, axis=0)[tail:, :]
            oc_ref[:, lo:hi] = (conv * _sigmoid(conv)).astype(oc_ref.dtype)
        og_ref[:, p * step:(p + 1) * step] = jnp.dot(
            x, wg_ref[:, p * step:(p + 1) * step], preferred_element_type=F32).astype(og_ref.dtype)


def _project(x, m, w_a, w_b, w_s, w_g, conv_w, seq):
    k = x.shape[1]
    tm = PROJ_TM
    qkv = 3 * DN_WIDTH
    widths = (w_a.shape[1], qkv, w_b.shape[1] - qkv, w_s.shape[1], w_g.shape[1])
    dtypes = (BF16, BF16, BF16, F32, BF16)
    const = lambda a: pl.BlockSpec(a.shape, lambda i: (0, 0))
    return pl.pallas_call(
        functools.partial(_proj_kernel, tm=tm, tiles_per_seq=seq // tm),
        grid=(m // tm,),
        in_specs=[pl.BlockSpec((tm, k), lambda i: (i, 0)), const(w_a), const(w_b), const(w_s), const(w_g),
                  const(conv_w)],
        out_specs=[pl.BlockSpec((tm, n), lambda i: (i, 0)) for n in widths],
        out_shape=[jax.ShapeDtypeStruct((m, n), dt) for n, dt in zip(widths, dtypes)],
        scratch_shapes=[pltpu.VMEM((tm + 8, qkv), F32)],
        compiler_params=_params("arbitrary"),
        name="proj",
    )(x, w_a, w_b, w_s, w_g, conv_w)


def _sb_kernel(q_ref, k_ref, v_ref, later_ref, o_ref, acc_ref, c_ref, *, tq, tk, sub):
    i = pl.program_id(2)
    lane = lax.broadcasted_iota(jnp.int32, (1, LANES), 1)
    head_a = lane < SB_HEAD_DIM
    row1 = lax.broadcasted_iota(jnp.int32, (tq, 1), 0)
    col = lax.broadcasted_iota(jnp.int32, (tq, tk), 1)
    col_b = col.astype(BF16)
    later = later_ref[...]
    tiles = range(sub)
    pairs = [(s, h) for s in tiles for h in range(2)]
    q_ends = [(i * sub + s + 1) * tq for s in tiles]
    q_heads = []
    for s in tiles:
        q = q_ref[s * tq:(s + 1) * tq, :] * jnp.asarray(SB_HEAD_DIM ** -0.5, q_ref.dtype)
        zero = jnp.zeros_like(q)
        q_heads.append((jnp.where(head_a, q, zero), jnp.where(head_a, zero, q)))

    def walk(b, first):
        nominal = [qe - (b + 1) * tk for qe in q_ends]
        starts = [pl.multiple_of(jnp.maximum(nm, 0), tq) for nm in nominal]
        ks = [k_ref[pl.ds(st, tk), :] for st in starts]
        vs = [v_ref[pl.ds(st, tk), :] for st in starts]
        bounds = [jnp.minimum(qe - tq + row1 - st, nm + tk - st) for st, qe, nm in zip(starts, q_ends, nominal)]
        valids = [col < bd for bd in bounds]
        valids_b = [col_b < bd.astype(BF16) for bd in bounds]
        zs = [lax.dot_general(q_heads[s][h], ks[s], NT_DIMS, preferred_element_type=F32) for s, h in pairs]
        zbs = [z.astype(BF16) for z in zs]
        log_betas = [jnp.minimum(zb, 0) - jnp.log(1 + jnp.exp(-jnp.abs(zb))) for zb in zbs]
        log_keeps = [jnp.where(valids_b[s], lb - zb, 0) for (s, _), lb, zb in zip(pairs, log_betas, zbs)]
        sums = [jnp.dot(lk, later, preferred_element_type=F32) for lk in log_keeps]
        weights = [jnp.where(valids[s], jnp.exp(lb.astype(F32) + sm), 0.0).astype(BF16)
                   for (s, _), lb, sm in zip(pairs, log_betas, sums)]
        totals = [sm[:, 0:1] + lk[:, 0:1].astype(F32) for sm, lk in zip(sums, log_keeps)]
        pvs = [jnp.dot(a, vs[s], preferred_element_type=F32) for (s, _), a in zip(pairs, weights)]
        cmax = [None] * sub
        for (s, h), pv, tot in zip(pairs, pvs, totals):
            if first:
                acc_ref[s, h] = pv
                c = jnp.broadcast_to(tot, (tq, LANES))
            else:
                c = c_ref[s, h]
                acc_ref[s, h] += jnp.exp(c) * pv
                c = c + tot
            c_ref[s, h] = c
            m = jnp.max(c)
            cmax[s] = m if cmax[s] is None else jnp.maximum(cmax[s], m)
        return cmax

    def unfinished(carry):
        b = carry[0]
        need = [(qe - b * tk > 0) & (cm > SB_SKIP_LOG) for qe, cm in zip(q_ends, carry[1:])]
        return functools.reduce(jnp.logical_or, need)

    lax.while_loop(unfinished, lambda carry: (carry[0] + 1, *walk(carry[0], False)),
                   (jnp.int32(1), *walk(0, True)))
    for s in tiles:
        o_ref[s * tq:(s + 1) * tq, :] = jnp.where(head_a, acc_ref[s, 0], acc_ref[s, 1]).astype(o_ref.dtype)


def _stick_breaking(qkv, batch, seq):
    pairs = SB_WIDTH // LANES
    tq, tk, sub = SB_TQ, SB_TK, SB_SUB
    rows = tq * sub
    later = (jnp.arange(tk)[:, None] > jnp.arange(tk)[None, :]).astype(BF16)
    return pl.pallas_call(
        functools.partial(_sb_kernel, tq=tq, tk=tk, sub=sub),
        grid=(batch, pairs, seq // rows),
        in_specs=[pl.BlockSpec((None, rows, LANES), lambda b, p, i: (b, i, p)),
                  pl.BlockSpec((None, seq, LANES), lambda b, p, i: (b, 0, pairs + p)),
                  pl.BlockSpec((None, seq, LANES), lambda b, p, i: (b, 0, 2 * pairs + p)),
                  pl.BlockSpec((tk, tk), lambda b, p, i: (0, 0))],
        out_specs=pl.BlockSpec((None, rows, LANES), lambda b, p, i: (b, i, p)),
        out_shape=jax.ShapeDtypeStruct((batch, seq, SB_WIDTH), BF16),
        scratch_shapes=[pltpu.VMEM((sub, 2, tq, LANES), F32), pltpu.VMEM((sub, 2, tq, LANES), F32)],
        compiler_params=_params("parallel", "parallel", "parallel"),
        name="stick_breaking",
    )(qkv, qkv, qkv, later)


CAT = DN_HEADS * CHUNK


def _split3(x):
    hi = x.astype(BF16)
    r = x - hi.astype(F32)
    mid = r.astype(BF16)
    return hi, mid, (r - mid.astype(F32)).astype(BF16)


def _block_diag(x, mask):
    return jnp.concatenate([x] * DN_HEADS, axis=0) * mask


def _cat_matmul(l, r, mask):
    return jnp.dot(l.astype(BF16), _block_diag(r.astype(BF16), mask), preferred_element_type=F32)


def _by_head(head, vals):
    return jnp.where(head == 0, vals[0], jnp.where(head == 1, vals[1], jnp.where(head == 2, vals[2], vals[3])))


def _gdn_constants():
    i = jnp.arange(CHUNK)
    tri = (i[:, None] >= i[None, :]).astype(BF16)
    row_head = jnp.arange(CAT)[:, None] // CHUNK
    wide_head = jnp.arange(DN_WIDTH) // DN_HEAD_DIM
    return (jnp.concatenate([tri] * 3, axis=1),
            (row_head == jnp.arange(CAT)[None, :] // CHUNK).astype(BF16),
            (row_head == wide_head[None, :]).astype(BF16),
            (row_head == jnp.concatenate([wide_head, wide_head])[None, :]).astype(BF16))


def _gdn_kernel(qkv_ref, z_ref, s_ref, alog_ref, dt_ref, og_ref, tri3_ref, bd_ref, bdk_ref,
                bdr_ref, o_ref, state_ref, sbd_ref, *, ts):
    @pl.when(pl.program_id(1) == 0)
    def _():
        state_ref[...] = jnp.zeros_like(state_ref)
        sbd_ref[...] = jnp.zeros_like(sbd_ref)

    rows = lax.broadcasted_iota(jnp.int32, (CHUNK, CAT), 0)
    lane = lax.broadcasted_iota(jnp.int32, (CHUNK, CAT), 1)
    cols = lane % CHUNK
    head = lane // CHUNK
    incl = rows >= cols
    strict = rows > cols
    upper = rows <= cols
    pair = rows // 2 == cols // 2
    lower_blocks = []
    blk = 2
    while blk < CHUNK:
        lower_blocks.append((rows // (2 * blk) == cols // (2 * blk)) & (rows // blk != cols // blk))
        blk *= 2
    neg_a = -jnp.exp(alog_ref[...])
    dt_bias = dt_ref[...]
    onorm_g = og_ref[...]
    heads = range(DN_HEADS)

    def operands(r0, beta_all, gcol):
        qs, ks, q_decs, k_decs, rhs_v, rhs_k, lasts = [], [], [], [], [], [], []
        for h in heads:
            lo, hi = h * DN_HEAD_DIM, (h + 1) * DN_HEAD_DIM
            q = qkv_ref[pl.ds(r0, CHUNK), lo:hi].astype(F32)
            k = qkv_ref[pl.ds(r0, CHUNK), DN_WIDTH + lo:DN_WIDTH + hi].astype(F32)
            v = qkv_ref[pl.ds(r0, CHUNK), 2 * DN_WIDTH + lo:2 * DN_WIDTH + hi].astype(F32)
            q = q * lax.rsqrt(jnp.sum(q * q, axis=-1, keepdims=True) + NORM_EPS) * (DN_HEAD_DIM ** -0.5)
            k = k * lax.rsqrt(jnp.sum(k * k, axis=-1, keepdims=True) + NORM_EPS)
            beta = beta_all[:, h:h + 1]
            gc = gcol[:, h * CHUNK:h * CHUNK + 1]
            glast = gc[CHUNK - 1:CHUNK, :]
            eg = jnp.exp(gc)
            qs.append(q)
            ks.append(k)
            q_decs.append(q * eg)
            k_decs.append((k * jnp.exp(glast - gc)).astype(BF16))
            rhs_v.append(beta * v)
            rhs_k.append(beta * eg * k)
            lasts.append(jnp.exp(glast))
        return (jnp.concatenate(ks, axis=1).astype(BF16), jnp.concatenate(qs, axis=1).astype(BF16),
                jnp.concatenate(rhs_v + rhs_k, axis=1), jnp.concatenate(q_decs, axis=1).astype(BF16),
                k_decs, lasts)

    def group_body(gi, _):
        group = range(GDN_GROUP)
        r0s = [pl.multiple_of((gi * GDN_GROUP + j) * CHUNK, CHUNK) for j in group]
        sms = [s_ref[pl.ds(r0, CHUNK), :] for r0 in r0s]
        beta_alls = [_sigmoid(sm) for sm in sms]
        g_alls = [neg_a * _softplus(sm + dt_bias) for sm in sms]
        beta_cats = [_by_head(head, [b[:, h:h + 1] for h in heads]) for b in beta_alls]
        g_cats = [_by_head(head, [g[:, DN_HEADS + h:DN_HEADS + h + 1] for h in heads]) for g in g_alls]
        gcols = [jnp.dot(tri3_ref[...], jnp.concatenate(_split3(g), axis=0), preferred_element_type=F32)
                 for g in g_cats]
        grows = [jnp.sum(jnp.where(upper, g, 0.0), axis=0, keepdims=True) for g in g_cats]
        gammas = [jnp.exp(jnp.where(incl, gc - gr, -jnp.inf)) for gc, gr in zip(gcols, grows)]
        ops = [operands(r0, b, gc) for r0, b, gc in zip(r0s, beta_alls, gcols)]
        bdk = bdk_ref[...]
        kqs = [lax.dot_general(jnp.concatenate([kc, qc], axis=0), _block_diag(kc, bdk), NT_DIMS,
                               preferred_element_type=F32) for kc, qc, _, _, _, _ in ops]
        qks = [(kq[CHUNK:] * gm).astype(BF16) for kq, gm in zip(kqs, gammas)]
        a_lows = [jnp.where(strict, b * kq[:CHUNK] * gm, 0.0) for b, kq, gm in zip(beta_cats, kqs, gammas)]
        bd = bd_ref[...]
        ys = [-jnp.where(pair, a, 0.0) for a in a_lows]
        for lower in lower_blocks:
            lows = [jnp.where(lower, a, 0.0) for a in a_lows]
            ts_ = [low + _cat_matmul(low, y, bd) for low, y in zip(lows, ys)]
            ys = [y - t - _cat_matmul(y, t, bd) for y, t in zip(ys, ts_)]
        bdr = bdr_ref[...]
        uws = [op[2] + _cat_matmul(y, op[2], bdr) for y, op in zip(ys, ops)]
        for j in group:
            _, _, _, q_dec, k_decs, lasts = ops[j]
            u, w = uws[j][:, :DN_WIDTH], uws[j][:, DN_WIDTH:]
            ws = jnp.dot(jnp.concatenate([w.astype(BF16), q_dec], axis=0), sbd_ref[...],
                         preferred_element_type=F32)
            vb = (u - ws[:CHUNK]).astype(BF16)
            o = ws[CHUNK:] + jnp.dot(qks[j], _block_diag(vb, bdk), preferred_element_type=F32)
            for h in heads:
                lo, hi = h * DN_HEAD_DIM, (h + 1) * DN_HEAD_DIM
                new = lasts[h] * state_ref[h] + lax.dot_general(k_decs[h], vb[:, lo:hi], TN_DIMS,
                                                                preferred_element_type=F32)
                state_ref[h] = new
                sbd_ref[lo:hi, lo:hi] = new.astype(BF16)
                oh = o[:, lo:hi]
                oh = oh * lax.rsqrt(jnp.mean(oh * oh, axis=-1, keepdims=True) + NORM_EPS) * onorm_g
                zz = z_ref[pl.ds(r0s[j], CHUNK), lo:hi].astype(F32)
                o_ref[pl.ds(r0s[j], CHUNK), lo:hi] = (oh * (zz * _sigmoid(zz))).astype(o_ref.dtype)
        return 0

    lax.fori_loop(0, ts // (CHUNK * GDN_GROUP), group_body, 0)


def _gated_deltanet(pc, pz, ps, alog_row, dt_row, onorm_row, batch, seq):
    ts = GDN_TS
    consts = _gdn_constants()
    const = lambda a: pl.BlockSpec(a.shape, lambda b, s: (0, 0))
    tile = lambda width: pl.BlockSpec((None, ts, width), lambda b, s: (b, s, 0))
    return pl.pallas_call(
        functools.partial(_gdn_kernel, ts=ts),
        grid=(batch, seq // ts),
        in_specs=[tile(3 * DN_WIDTH), tile(DN_WIDTH), tile(LANES), const(alog_row), const(dt_row),
                  const(onorm_row)] + [const(a) for a in consts],
        out_specs=tile(DN_WIDTH),
        out_shape=jax.ShapeDtypeStruct((batch, seq, DN_WIDTH), BF16),
        scratch_shapes=[pltpu.VMEM((DN_HEADS, DN_HEAD_DIM, DN_HEAD_DIM), F32),
                        pltpu.VMEM((DN_WIDTH, DN_WIDTH), BF16)],
        compiler_params=_params("parallel", "arbitrary"),
        name="gated_deltanet",
    )(pc, pz, ps, alog_row, dt_row, onorm_row, *consts)


def _top2_of4(vals):
    m1 = jnp.maximum(jnp.maximum(vals[0], vals[1]), jnp.maximum(vals[2], vals[3]))
    i1 = jnp.where(vals[0] == m1, 0, jnp.where(vals[1] == m1, 1, jnp.where(vals[2] == m1, 2, 3)))
    rest = [jnp.where(i1 == r, -jnp.inf, vals[r]) for r in range(4)]
    m2 = jnp.maximum(jnp.maximum(rest[0], rest[1]), jnp.maximum(rest[2], rest[3]))
    i2 = jnp.where(rest[0] == m2, 0, jnp.where(rest[1] == m2, 1, jnp.where(rest[2] == m2, 2, 3)))
    return m1, i1, m2, i2


def _pick4(idx, vals):
    return jnp.where(idx == 0, vals[0], jnp.where(idx == 1, vals[1], jnp.where(idx == 2, vals[2], vals[3])))


def _merge_kernel(oa_ref, ob_ref, ga_ref, gb_ref, x_ref, wa_ref, wb_ref, wo_ref, g_ref, b_ref,
                  wr_ref, rb_ref, h_ref, ri_ref, cnt_ref, carry_ref, *, tm, parts, alpha):
    t = pl.program_id(0)

    @pl.when(t == 0)
    def _():
        carry_ref[...] = jnp.zeros_like(carry_ref)

    pm = tm // parts
    subs = [slice(p * pm, (p + 1) * pm) for p in range(parts)]
    each = lambda f, *lists: [f(*args) for args in zip(*lists)]
    ya = [jnp.dot(oa_ref[r, :], wa_ref[...], preferred_element_type=F32) for r in subs]
    yb = [jnp.dot(ob_ref[r, :], wb_ref[...], preferred_element_type=F32) for r in subs]
    merged = [(_sigmoid(ga_ref[r, :].astype(F32)) * a + _sigmoid(gb_ref[r, :].astype(F32)) * b).astype(BF16)
              for r, a, b in zip(subs, ya, yb)]
    mix = [jnp.dot(m, wo_ref[...], preferred_element_type=F32) for m in merged]
    hs = [_layer_norm(alpha * x_ref[r, :] + mx, g_ref[...], b_ref[...]) for r, mx in zip(subs, mix)]
    for r, h in zip(subs, hs):
        h_ref[r, :D_MODEL] = h

    logits = [jnp.dot(h.astype(BF16), wr_ref[...], preferred_element_type=F32) for h in hs]
    lts = [lg.T[:N_EXPERTS, :] for lg in logits]
    exs = [jnp.exp(lt - jnp.max(lt, axis=0, keepdims=True)) for lt in lts]
    probs = [ex / jnp.sum(ex, axis=0, keepdims=True) for ex in exs]
    sels = [p + rb_ref[...] for p in probs]
    sel_rows = [[sel[e:e + 1, :] for e in range(N_EXPERTS)] for sel in sels]
    prob_rows = [[p[e:e + 1, :] for e in range(N_EXPERTS)] for p in probs]
    groups = range(N_GROUPS)
    local = range(EXPERTS_PER_GROUP)
    scores = [[sum(_top2_of4(rows[g * EXPERTS_PER_GROUP:(g + 1) * EXPERTS_PER_GROUP])[0:3:2]) for g in groups]
              for rows in sel_rows]
    best = [jnp.maximum(jnp.maximum(sc[0], sc[1]), jnp.maximum(sc[2], sc[3])) for sc in scores]
    grps = [jnp.where(sc[0] == bs, 0, jnp.where(sc[1] == bs, 1, jnp.where(sc[2] == bs, 2, 3)))
            for sc, bs in zip(scores, best)]
    in_grp = [[_pick4(grp, [rows[g * EXPERTS_PER_GROUP + r] for g in groups]) for r in local]
              for grp, rows in zip(grps, sel_rows)]
    p_grp = [[_pick4(grp, [rows[g * EXPERTS_PER_GROUP + r] for g in groups]) for r in local]
             for grp, rows in zip(grps, prob_rows)]
    tops = [_top2_of4(ig) for ig in in_grp]
    l0s = [tp[1] for tp in tops]
    l1s = [tp[3] for tp in tops]
    p0s = each(_pick4, l0s, p_grp)
    p1s = each(_pick4, l1s, p_grp)
    w0s = [p0 / (p0 + p1) for p0, p1 in zip(p0s, p1s)]
    w1s = [p1 / (p0 + p1) for p0, p1 in zip(p0s, p1s)]
    r128 = lax.broadcasted_iota(jnp.int32, (LANES, pm), 0)
    for r, l0, l1, w0, w1 in zip(subs, l0s, l1s, w0s, w1s):
        local_w = jnp.zeros((LANES, pm), F32)
        for e in local:
            local_w = jnp.where(r128 == e, jnp.where(l0 == e, w0, jnp.where(l1 == e, w1, 0.0)), local_w)
        h_ref[r, D_MODEL:] = local_w.T

    firsts = each(jnp.minimum, l0s, l1s)
    seconds = each(jnp.maximum, l0s, l1s)
    pair_ids = [jnp.where(f == 0, sd - 1, jnp.where(f == 1, sd + 1, PAIRS_PER_GROUP - 1))
                for f, sd in zip(firsts, seconds)]
    clss = [grp * PAIRS_PER_GROUP + pid for grp, pid in zip(grps, pair_ids)]
    cidx = lax.broadcasted_iota(jnp.int32, (CLASS_ROWS, pm), 0)
    hits = [cidx == cls for cls in clss]
    onehots = [jnp.where(hit, 1.0, 0.0) for hit in hits]
    trow = lax.broadcasted_iota(jnp.int32, (pm, pm), 0)
    tcol = lax.broadcasted_iota(jnp.int32, (pm, pm), 1)
    earlier = jnp.where(trow < tcol, 1.0, 0.0).astype(BF16)
    inside = [jnp.dot(oh.astype(BF16), earlier, preferred_element_type=F32) for oh in onehots]
    totals = [jnp.sum(oh, axis=1, keepdims=True) for oh in onehots]
    carry = carry_ref[...]
    r8 = lax.broadcasted_iota(jnp.int32, (8, pm), 0)
    for r, cls, hit, ins, tot in zip(subs, clss, hits, inside, totals):
        rank = jnp.sum(jnp.where(hit, ins + carry, 0.0), axis=0, keepdims=True).astype(jnp.int32)
        ri_ref[:, r] = jnp.where(r8 == 0, cls, jnp.where(r8 == 1, rank, 0))
        carry = carry + tot
    carry_ref[...] = carry
    cnt_ref[...] = jnp.broadcast_to(carry, cnt_ref.shape).astype(jnp.int32)


def _merge(o_a, o_b, gates, x, w_br_a, w_br_b, w_o, ln_g, ln_b, w_router, router_bias, alpha):
    n = o_a.shape[0]
    tm = MERGE_TM
    full = lambda shape: pl.BlockSpec(shape, lambda t: (0,) * len(shape))
    return pl.pallas_call(
        functools.partial(_merge_kernel, tm=tm, parts=MERGE_PARTS, alpha=alpha),
        grid=(n // tm,),
        in_specs=[pl.BlockSpec((tm, SB_WIDTH), lambda t: (t, 0)),
                  pl.BlockSpec((tm, DN_WIDTH), lambda t: (t, 0)),
                  pl.BlockSpec((tm, D_MODEL), lambda t: (t, 0)),
                  pl.BlockSpec((tm, D_MODEL), lambda t: (t, 1)),
                  pl.BlockSpec((tm, D_MODEL), lambda t: (t, 0)),
                  full((SB_WIDTH, D_MODEL)), full((DN_WIDTH, D_MODEL)), full((D_MODEL, D_MODEL)),
                  full((1, D_MODEL)), full((1, D_MODEL)),
                  full((D_MODEL, LANES)), full((N_EXPERTS, 1))],
        out_specs=[pl.BlockSpec((tm, H_EXT), lambda t: (t, 0)),
                   pl.BlockSpec((8, tm), lambda t: (0, t)),
                   pl.BlockSpec((CLASS_ROWS, LANES), lambda t: (0, 0))],
        out_shape=[jax.ShapeDtypeStruct((n, H_EXT), F32),
                   jax.ShapeDtypeStruct((8, n), jnp.int32),
                   jax.ShapeDtypeStruct((CLASS_ROWS, LANES), jnp.int32)],
        scratch_shapes=[pltpu.VMEM((CLASS_ROWS, 1), F32)],
        compiler_params=_params("arbitrary"),
        name="merge_ln_route",
    )(o_a, o_b, gates, gates, x, w_br_a, w_br_b, w_o, ln_g, ln_b, w_router, router_bias)


def _group_ffn_kernel(blk_grp_ref, src_ref, dst_ref, used_ref, n_used_ref, h_hbm, wg_ref, wu_ref, wd_ref,
                      g_ref, b_ref, f_hbm, xbuf, obuf, gsem, ssem, *, rb, n_blk, alpha):
    del blk_grp_ref
    i = pl.program_id(0)
    slot = i % 2
    other = 1 - slot

    def gather_row(block, r, s):
        return pltpu.make_async_copy(h_hbm.at[pl.ds(src_ref[block * rb + r], 1), :],
                                     xbuf.at[s, pl.ds(r, 1), :], gsem.at[s])

    def scatter_row(block, r, s):
        return pltpu.make_async_copy(obuf.at[s, pl.ds(r, 1), :],
                                     f_hbm.at[pl.ds(dst_ref[(block + 2) * rb + r], 1), :], ssem.at[s])

    @pl.when(i == 0)
    def _():
        obuf[...] = jnp.zeros_like(obuf)

        def issue(r, _):
            gather_row(0, r, 0).start()
            scatter_row(-2, r, 1).start()
            return 0

        lax.fori_loop(0, rb, issue, 0)
        pltpu.make_async_copy(obuf.at[1], f_hbm.at[pl.ds(0, rb), :], ssem.at[1]).wait()

    pltpu.make_async_copy(h_hbm.at[pl.ds(0, rb), :], xbuf.at[slot], gsem.at[slot]).wait()

    @pl.when(i > 0)
    def _():
        pltpu.make_async_copy(obuf.at[slot], f_hbm.at[pl.ds(0, rb), :], ssem.at[slot]).wait()

    @pl.when(i < n_blk)
    def _():
        x = xbuf[slot, :, :D_MODEL].astype(BF16)
        w_cols = [xbuf[slot, :, D_MODEL + e:D_MODEL + e + 1] for e in range(EXPERTS_PER_GROUP)]

        def apply(e):
            gate = jnp.dot(x, wg_ref[e], preferred_element_type=F32)
            up = jnp.dot(x, wu_ref[e], preferred_element_type=F32)
            hid = (gate * _sigmoid(gate) * up).astype(BF16)
            return w_cols[e] * jnp.dot(hid, wd_ref[e], preferred_element_type=F32)

        part = rb // 2
        for k in range(EXPERTS_PER_GROUP):
            for e in range(k, min(k + 3, EXPERTS_PER_GROUP)):
                @pl.when((used_ref[i * EXPERTS_PER_GROUP + k] == e) & (n_used_ref[i] > k))
                def _():
                    if k < 2:
                        for r in range(k * part, (k + 1) * part):
                            gather_row(i + 1, r, other).start(priority=0)
                            scatter_row(i - 1, r, other).start(priority=1)
                    if k == 0:
                        obuf[slot] = apply(e)
                    else:
                        obuf[slot] += apply(e)

        obuf[slot] = _layer_norm(alpha * xbuf[slot, :, :D_MODEL] + obuf[slot], g_ref[...], b_ref[...])

    @pl.when(i == n_blk)
    def _():
        def issue(r, _):
            scatter_row(n_blk - 1, r, other).start()
            return 0

        lax.fori_loop(0, rb, issue, 0)
        pltpu.make_async_copy(obuf.at[other], f_hbm.at[pl.ds(0, rb), :], ssem.at[other]).wait()


def _group_ffn(h_ext, blk_grp, row_src, row_dst, used, n_used, w_gate, w_up, w_down, ln_g, ln_b, alpha, n_tok):
    rb = ROUTE_BLOCK
    n_blk = row_src.shape[0] // rb - 1
    wspec = lambda a, b: pl.BlockSpec((EXPERTS_PER_GROUP, a, b), lambda i, bg, rs, rd, us, nu: (bg[i], 0, 0))
    vec = pl.BlockSpec((1, D_MODEL), lambda i, bg, rs, rd, us, nu: (0, 0))
    return pl.pallas_call(
        functools.partial(_group_ffn_kernel, rb=rb, n_blk=n_blk, alpha=alpha),
        grid_spec=pltpu.PrefetchScalarGridSpec(
            num_scalar_prefetch=5,
            grid=(n_blk + 1,),
            in_specs=[pl.BlockSpec(memory_space=pl.ANY),
                      wspec(D_MODEL, D_EXPERT), wspec(D_MODEL, D_EXPERT), wspec(D_EXPERT, D_MODEL),
                      vec, vec],
            out_specs=pl.BlockSpec(memory_space=pl.ANY),
            scratch_shapes=[pltpu.VMEM((2, rb, H_EXT), F32), pltpu.VMEM((2, rb, D_MODEL), F32),
                            pltpu.SemaphoreType.DMA((2,)), pltpu.SemaphoreType.DMA((2,))]),
        out_shape=jax.ShapeDtypeStruct((n_tok + 2 * rb, D_MODEL), F32),
        compiler_params=_params("arbitrary"),
        name="group_ffn",
    )(blk_grp, row_src, row_dst, used, n_used, h_ext, w_gate, w_up, w_down, ln_g, ln_b)


def _route_rows(ri, cnt, n_tok):
    rb = ROUTE_BLOCK
    n_cls = N_GROUPS * PAIRS_PER_GROUP
    c_count = cnt[:n_cls, 0].reshape(N_GROUPS, PAIRS_PER_GROUP)
    g_count = jnp.sum(c_count, axis=1)
    g_padded = (g_count + rb - 1) // rb * rb
    g_end = jnp.cumsum(g_padded)
    c_start = ((g_end - g_padded)[:, None] + jnp.cumsum(c_count, axis=1) - c_count).reshape(n_cls)
    c_count = c_count.reshape(n_cls)
    dest = c_start[ri[0]] + ri[1]
    p_rows = n_tok + N_GROUPS * rb
    row_tok = jnp.full((p_rows,), -1, jnp.int32).at[dest].set(jnp.arange(n_tok, dtype=jnp.int32))
    spare = n_tok + jnp.arange(-2 * rb, p_rows, dtype=jnp.int32) % (2 * rb)
    row_src = jnp.concatenate([jnp.maximum(row_tok, 0), jnp.zeros((rb,), jnp.int32)])
    row_dst = jnp.concatenate([jnp.full((2 * rb,), -1, jnp.int32), row_tok])
    row_dst = jnp.where(row_dst < 0, spare, row_dst)
    blk_start = jnp.arange(p_rows // rb + 1, dtype=jnp.int32) * rb
    blk_grp = jnp.minimum(jnp.searchsorted(g_end, blk_start, side="right"), N_GROUPS - 1).astype(jnp.int32)
    pairs = [(a, b) for a in range(EXPERTS_PER_GROUP) for b in range(a + 1, EXPERTS_PER_GROUP)]
    member = jnp.array([[e in pairs[c % PAIRS_PER_GROUP] for e in range(EXPERTS_PER_GROUP)]
                        for c in range(n_cls)])
    overlap = ((c_start[None, :] < blk_start[:, None] + rb) & ((c_start + c_count)[None, :] > blk_start[:, None])
               & (c_count[None, :] > 0))
    need = jnp.any(overlap[:, :, None] & member[None, :, :], axis=1)
    used = jnp.argsort(~need, axis=1, stable=True).astype(jnp.int32)
    n_used = jnp.maximum(jnp.sum(need, axis=1), 2).astype(jnp.int32)
    return row_src, row_dst, blk_grp, used.reshape(-1), n_used


def _lane_row(values, offset):
    return jnp.zeros((1, LANES), F32).at[0, offset:offset + values.shape[0]].set(values.astype(F32))


def kernel(x, w_in, conv_w, a_log, dt_bias, onorm_g, w_br_a, w_br_b, w_o, ln1_g, ln1_b, w_router,
           router_bias, w_gate, w_up, w_down, ln2_g, ln2_b):
    batch, seq, d = x.shape
    depth = w_in.shape[0]
    n_tok = batch * seq
    alpha = (2 * depth) ** 0.25
    a_end = 3 * SB_WIDTH
    b_end = a_end + 4 * DN_WIDTH
    s_end = b_end + 2 * DN_HEADS
    w_a = w_in[:, :, :a_end].astype(BF16)
    w_b = w_in[:, :, a_end:b_end].astype(BF16)
    w_s = jnp.pad(w_in[:, :, b_end:s_end], ((0, 0), (0, 0), (0, LANES - 2 * DN_HEADS))).astype(BF16)
    w_g = w_in[:, :, s_end:].astype(BF16)
    w_r = jnp.pad(w_router, ((0, 0), (0, LANES - N_EXPERTS))).astype(BF16)
    r_bias = router_bias.astype(F32).reshape(N_EXPERTS, 1)
    w_br_a, w_br_b, w_o = w_br_a.astype(BF16), w_br_b.astype(BF16), w_o.astype(BF16)
    w_gate, w_up, w_down = w_gate.astype(BF16), w_up.astype(BF16), w_down.astype(BF16)

    xf = x.reshape(n_tok, d)
    for l in range(depth):
        pa, pc, pz, ps, gates = _project(xf, n_tok, w_a[l], w_b[l], w_s[l], w_g[l], conv_w[l].astype(F32), seq)
        pa = pa.reshape(batch, seq, a_end)
        pc = pc.reshape(batch, seq, 3 * DN_WIDTH)
        pz = pz.reshape(batch, seq, DN_WIDTH)
        ps = ps.reshape(batch, seq, LANES)
        o_a = _stick_breaking(pa, batch, seq).reshape(n_tok, SB_WIDTH)
        o_b = _gated_deltanet(pc, pz, ps, _lane_row(a_log[l], DN_HEADS), _lane_row(dt_bias[l], DN_HEADS),
                              onorm_g[l].astype(F32).reshape(1, DN_HEAD_DIM), batch, seq).reshape(n_tok, DN_WIDTH)
        h_ext, ri, cnt = _merge(o_a, o_b, gates, xf, w_br_a[l], w_br_b[l], w_o[l],
                                ln1_g[l].reshape(1, d), ln1_b[l].reshape(1, d), w_r, r_bias, alpha)
        row_src, row_dst, blk_grp, used, n_used = _route_rows(ri, cnt, n_tok)
        xf = _group_ffn(h_ext, blk_grp, row_src, row_dst, used, n_used, w_gate[l], w_up[l], w_down[l],
                        ln2_g[l].reshape(1, d), ln2_b[l].reshape(1, d), alpha, n_tok)
    return xf[:n_tok].reshape(batch, seq, d)
```

```python
import functools

import jax
import jax.numpy as jnp
from jax import lax
from jax.experimental import pallas as pl
from jax.experimental.pallas import tpu as pltpu

D_MODEL = 1024
SB_HEADS = 8
SB_HEAD_DIM = 64
SB_WIDTH = SB_HEADS * SB_HEAD_DIM
DN_HEADS = 4
DN_HEAD_DIM = 128
DN_WIDTH = DN_HEADS * DN_HEAD_DIM
CONV_K = 4
CHUNK = 64
N_EXPERTS = 16
N_GROUPS = 4
EXPERTS_PER_GROUP = N_EXPERTS // N_GROUPS
PAIRS_PER_GROUP = EXPERTS_PER_GROUP * (EXPERTS_PER_GROUP - 1) // 2
CLASS_ROWS = 32
D_EXPERT = 512
LN_EPS = 1e-5
NORM_EPS = 1e-6

LANES = 128
H_EXT = D_MODEL + LANES
VMEM_LIMIT = 48 * 1024 * 1024

PROJ_TM = 512
SB_TQ = 128
SB_TK = 256
SB_SUB = 16
GDN_TS = 512
GDN_GROUP = 8
MERGE_TM = 1024
MERGE_PARTS = 4
ROUTE_BLOCK = 256
SB_SKIP_LOG = -60.0

BF16 = jnp.bfloat16
F32 = jnp.float32
NT_DIMS = (((1,), (1,)), ((), ()))
TN_DIMS = (((0,), (0,)), ((), ()))


def _params(*semantics):
    return pltpu.CompilerParams(dimension_semantics=semantics, vmem_limit_bytes=VMEM_LIMIT)


def _sigmoid(x):
    return 1.0 / (1.0 + jnp.exp(-x))


def _softplus(x):
    return jnp.maximum(x, 0.0) + jnp.log(1.0 + jnp.exp(-jnp.abs(x)))


def _layer_norm(y, g, b):
    mu = jnp.mean(y, axis=-1, keepdims=True)
    yc = y - mu
    var = jnp.mean(yc * yc, axis=-1, keepdims=True)
    return yc * lax.rsqrt(var + LN_EPS) * g + b


def _proj_kernel(x_ref, wa_ref, wb_ref, ws_ref, wg_ref, cw_ref, oa_ref, oc_ref, oz_ref, os_ref, og_ref,
                 ubuf, *, tm, tiles_per_seq):
    i = pl.program_id(0)
    tail = 8
    qkv = 3 * DN_WIDTH

    @pl.when(i == 0)
    def _():
        ubuf[tm:tm + tail, :] = jnp.zeros((tail, qkv), F32)

    x = x_ref[...].astype(BF16)
    oa_ref[...] = jnp.dot(x, wa_ref[...], preferred_element_type=F32).astype(oa_ref.dtype)
    pb = jnp.dot(x, wb_ref[...], preferred_element_type=F32)
    oz_ref[...] = pb[:, qkv:].astype(oz_ref.dtype)
    prev = ubuf[tm:tm + tail, :]
    ubuf[0:tail, :] = jnp.where(i % tiles_per_seq == 0, jnp.zeros_like(prev), prev)
    ubuf[tail:tail + tm, :] = pb[:, :qkv]
    os_ref[...] = jnp.dot(x, ws_ref[...], preferred_element_type=F32)
    gate_cols = wg_ref.shape[1]
    pieces = 4
    step = gate_cols // pieces
    lane_tiles = qkv // LANES
    for p in range(pieces):
        for c in range(p * lane_tiles // pieces, (p + 1) * lane_tiles // pieces):
            lo, hi = c * LANES, (c + 1) * LANES
            u = ubuf[:, lo:hi]
            conv = cw_ref[CONV_K - 1:CONV_K, lo:hi] * u[tail:, :]
            for j in range(CONV_K - 1):
                conv = conv + cw_ref[j:j + 1, lo:hi] * pltpu.roll(u, CONV_K - 1 - j, axis=0)[tail:, :]
            oc_ref[:, lo:hi] = (conv * _sigmoid(conv)).astype(oc_ref.dtype)
        og_ref[:, p * step:(p + 1) * step] = jnp.dot(
            x, wg_ref[:, p * step:(p + 1) * step], preferred_element_type=F32).astype(og_ref.dtype)


def _project(x, m, w_a, w_b, w_s, w_g, conv_w, seq):
    k = x.shape[1]
    tm = PROJ_TM
    qkv = 3 * DN_WIDTH
    widths = (w_a.shape[1], qkv, w_b.shape[1] - qkv, w_s.shape[1], w_g.shape[1])
    dtypes = (BF16, BF16, BF16, F32, BF16)
    const = lambda a: pl.BlockSpec(a.shape, lambda i: (0, 0))
    return pl.pallas_call(
        functools.partial(_proj_kernel, tm=tm, tiles_per_seq=seq // tm),
        grid=(m // tm,),
        in_specs=[pl.BlockSpec((tm, k), lambda i: (i, 0)), const(w_a), const(w_b), const(w_s), const(w_g),
                  const(conv_w)],
        out_specs=[pl.BlockSpec((tm, n), lambda i: (i, 0)) for n in widths],
        out_shape=[jax.ShapeDtypeStruct((m, n), dt) for n, dt in zip(widths, dtypes)],
        scratch_shapes=[pltpu.VMEM((tm + 8, qkv), F32)],
        compiler_params=_params("arbitrary"),
        name="proj",
    )(x, w_a, w_b, w_s, w_g, conv_w)


def _sb_kernel(q_ref, k_ref, v_ref, later_ref, o_ref, acc_ref, c_ref, *, tq, tk, sub):
    i = pl.program_id(2)
    lane = lax.broadcasted_iota(jnp.int32, (1, LANES), 1)
    head_a = lane < SB_HEAD_DIM
    row1 = lax.broadcasted_iota(jnp.int32, (tq, 1), 0)
    col = lax.broadcasted_iota(jnp.int32, (tq, tk), 1)
    col_b = col.astype(BF16)
    later = later_ref[...]
    tiles = range(sub)
    pairs = [(s, h) for s in tiles for h in range(2)]
    q_ends = [(i * sub + s + 1) * tq for s in tiles]
    q_heads = []
    for s in tiles:
        q = q_ref[s * tq:(s + 1) * tq, :] * jnp.asarray(SB_HEAD_DIM ** -0.5, q_ref.dtype)
        zero = jnp.zeros_like(q)
        q_heads.append((jnp.where(head_a, q, zero), jnp.where(head_a, zero, q)))

    def walk(b, first):
        nominal = [qe - (b + 1) * tk for qe in q_ends]
        starts = [pl.multiple_of(jnp.maximum(nm, 0), tq) for nm in nominal]
        ks = [k_ref[pl.ds(st, tk), :] for st in starts]
        vs = [v_ref[pl.ds(st, tk), :] for st in starts]
        bounds = [jnp.minimum(qe - tq + row1 - st, nm + tk - st) for st, qe, nm in zip(starts, q_ends, nominal)]
        valids = [col < bd for bd in bounds]
        valids_b = [col_b < bd.astype(BF16) for bd in bounds]
        zs = [lax.dot_general(q_heads[s][h], ks[s], NT_DIMS, preferred_element_type=F32) for s, h in pairs]
        zbs = [z.astype(BF16) for z in zs]
        log_betas = [jnp.minimum(zb, 0) - jnp.log(1 + jnp.exp(-jnp.abs(zb))) for zb in zbs]
        log_keeps = [jnp.where(valids_b[s], lb - zb, 0) for (s, _), lb, zb in zip(pairs, log_betas, zbs)]
        sums = [jnp.dot(lk, later, preferred_element_type=F32) for lk in log_keeps]
        weights = [jnp.where(valids[s], jnp.exp(lb.astype(F32) + sm), 0.0).astype(BF16)
                   for (s, _), lb, sm in zip(pairs, log_betas, sums)]
        totals = [sm[:, 0:1] + lk[:, 0:1].astype(F32) for sm, lk in zip(sums, log_keeps)]
        pvs = [jnp.dot(a, vs[s], preferred_element_type=F32) for (s, _), a in zip(pairs, weights)]
        cmax = [None] * sub
        for (s, h), pv, tot in zip(pairs, pvs, totals):
            if first:
                acc_ref[s, h] = pv
                c = jnp.broadcast_to(tot, (tq, LANES))
            else:
                c = c_ref[s, h]
                acc_ref[s, h] += jnp.exp(c) * pv
                c = c + tot
            c_ref[s, h] = c
            m = jnp.max(c)
            cmax[s] = m if cmax[s] is None else jnp.maximum(cmax[s], m)
        return cmax

    def unfinished(carry):
        b = carry[0]
        need = [(qe - b * tk > 0) & (cm > SB_SKIP_LOG) for qe, cm in zip(q_ends, carry[1:])]
        return functools.reduce(jnp.logical_or, need)

    lax.while_loop(unfinished, lambda carry: (carry[0] + 1, *walk(carry[0], False)),
                   (jnp.int32(1), *walk(0, True)))
    for s in tiles:
        o_ref[s * tq:(s + 1) * tq, :] = jnp.where(head_a, acc_ref[s, 0], acc_ref[s, 1]).astype(o_ref.dtype)


def _stick_breaking(qkv, batch, seq):
    pairs = SB_WIDTH // LANES
    tq, tk, sub = SB_TQ, SB_TK, SB_SUB
    rows = tq * sub
    later = (jnp.arange(tk)[:, None] > jnp.arange(tk)[None, :]).astype(BF16)
    return pl.pallas_call(
        functools.partial(_sb_kernel, tq=tq, tk=tk, sub=sub),
        grid=(batch, pairs, seq // rows),
        in_specs=[pl.BlockSpec((None, rows, LANES), lambda b, p, i: (b, i, p)),
                  pl.BlockSpec((None, seq, LANES), lambda b, p, i: (b, 0, pairs + p)),
                  pl.BlockSpec((None, seq, LANES), lambda b, p, i: (b, 0, 2 * pairs + p)),
                  pl.BlockSpec((tk, tk), lambda b, p, i: (0, 0))],
        out_specs=pl.BlockSpec((None, rows, LANES), lambda b, p, i: (b, i, p)),
        out_shape=jax.ShapeDtypeStruct((batch, seq, SB_WIDTH), BF16),
        scratch_shapes=[pltpu.VMEM((sub, 2, tq, LANES), F32), pltpu.VMEM((sub, 2, tq, LANES), F32)],
        compiler_params=_params("parallel", "parallel", "parallel"),
        name="stick_breaking",
    )(qkv, qkv, qkv, later)


CAT = DN_HEADS * CHUNK


def _split3(x):
    hi = x.astype(BF16)
    r = x - hi.astype(F32)
    mid = r.astype(BF16)
    return hi, mid, (r - mid.astype(F32)).astype(BF16)


def _block_diag(x, mask):
    return jnp.concatenate([x] * DN_HEADS, axis=0) * mask


def _cat_matmul(l, r, mask):
    return jnp.dot(l.astype(BF16), _block_diag(r.astype(BF16), mask), preferred_element_type=F32)


def _by_head(head, vals):
    return jnp.where(head == 0, vals[0], jnp.where(head == 1, vals[1], jnp.where(head == 2, vals[2], vals[3])))


def _gdn_constants():
    i = jnp.arange(CHUNK)
    tri = (i[:, None] >= i[None, :]).astype(BF16)
    row_head = jnp.arange(CAT)[:, None] // CHUNK
    wide_head = jnp.arange(DN_WIDTH) // DN_HEAD_DIM
    return (jnp.concatenate([tri] * 3, axis=1),
            (row_head == jnp.arange(CAT)[None, :] // CHUNK).astype(BF16),
            (row_head == wide_head[None, :]).astype(BF16),
            (row_head == jnp.concatenate([wide_head, wide_head])[None, :]).astype(BF16))


def _gdn_kernel(qkv_ref, z_ref, s_ref, alog_ref, dt_ref, og_ref, tri3_ref, bd_ref, bdk_ref,
                bdr_ref, o_ref, state_ref, sbd_ref, *, ts):
    @pl.when(pl.program_id(1) == 0)
    def _():
        state_ref[...] = jnp.zeros_like(state_ref)
        sbd_ref[...] = jnp.zeros_like(sbd_ref)

    rows = lax.broadcasted_iota(jnp.int32, (CHUNK, CAT), 0)
    lane = lax.broadcasted_iota(jnp.int32, (CHUNK, CAT), 1)
    cols = lane % CHUNK
    head = lane // CHUNK
    incl = rows >= cols
    strict = rows > cols
    upper = rows <= cols
    pair = rows // 2 == cols // 2
    lower_blocks = []
    blk = 2
    while blk < CHUNK:
        lower_blocks.append((rows // (2 * blk) == cols // (2 * blk)) & (rows // blk != cols // blk))
        blk *= 2
    neg_a = -jnp.exp(alog_ref[...])
    dt_bias = dt_ref[...]
    onorm_g = og_ref[...]
    heads = range(DN_HEADS)

    def operands(r0, beta_all, gcol):
        qs, ks, q_decs, k_decs, rhs_v, rhs_k, lasts = [], [], [], [], [], [], []
        for h in heads:
            lo, hi = h * DN_HEAD_DIM, (h + 1) * DN_HEAD_DIM
            q = qkv_ref[pl.ds(r0, CHUNK), lo:hi].astype(F32)
            k = qkv_ref[pl.ds(r0, CHUNK), DN_WIDTH + lo:DN_WIDTH + hi].astype(F32)
            v = qkv_ref[pl.ds(r0, CHUNK), 2 * DN_WIDTH + lo:2 * DN_WIDTH + hi].astype(F32)
            q = q * lax.rsqrt(jnp.sum(q * q, axis=-1, keepdims=True) + NORM_EPS) * (DN_HEAD_DIM ** -0.5)
            k = k * lax.rsqrt(jnp.sum(k * k, axis=-1, keepdims=True) + NORM_EPS)
            beta = beta_all[:, h:h + 1]
            gc = gcol[:, h * CHUNK:h * CHUNK + 1]
            glast = gc[CHUNK - 1:CHUNK, :]
            eg = jnp.exp(gc)
            qs.append(q)
            ks.append(k)
            q_decs.append(q * eg)
            k_decs.append((k * jnp.exp(glast - gc)).astype(BF16))
            rhs_v.append(beta * v)
            rhs_k.append(beta * eg * k)
            lasts.append(jnp.exp(glast))
        return (jnp.concatenate(ks, axis=1).astype(BF16), jnp.concatenate(qs, axis=1).astype(BF16),
                jnp.concatenate(rhs_v + rhs_k, axis=1), jnp.concatenate(q_decs, axis=1).astype(BF16),
                k_decs, lasts)

    def group_body(gi, _):
        group = range(GDN_GROUP)
        r0s = [pl.multiple_of((gi * GDN_GROUP + j) * CHUNK, CHUNK) for j in group]
        sms = [s_ref[pl.ds(r0, CHUNK), :] for r0 in r0s]
        beta_alls = [_sigmoid(sm) for sm in sms]
        g_alls = [neg_a * _softplus(sm + dt_bias) for sm in sms]
        beta_cats = [_by_head(head, [b[:, h:h + 1] for h in heads]) for b in beta_alls]
        g_cats = [_by_head(head, [g[:, DN_HEADS + h:DN_HEADS + h + 1] for h in heads]) for g in g_alls]
        gcols = [jnp.dot(tri3_ref[...], jnp.concatenate(_split3(g), axis=0), preferred_element_type=F32)
                 for g in g_cats]
        grows = [jnp.sum(jnp.where(upper, g, 0.0), axis=0, keepdims=True) for g in g_cats]
        gammas = [jnp.exp(jnp.where(incl, gc - gr, -jnp.inf)) for gc, gr in zip(gcols, grows)]
        ops = [operands(r0, b, gc) for r0, b, gc in zip(r0s, beta_alls, gcols)]
        bdk = bdk_ref[...]
        kqs = [lax.dot_general(jnp.concatenate([kc, qc], axis=0), _block_diag(kc, bdk), NT_DIMS,
                               preferred_element_type=F32) for kc, qc, _, _, _, _ in ops]
        qks = [(kq[CHUNK:] * gm).astype(BF16) for kq, gm in zip(kqs, gammas)]
        a_lows = [jnp.where(strict, b * kq[:CHUNK] * gm, 0.0) for b, kq, gm in zip(beta_cats, kqs, gammas)]
        bd = bd_ref[...]
        ys = [-jnp.where(pair, a, 0.0) for a in a_lows]
        for lower in lower_blocks:
            lows = [jnp.where(lower, a, 0.0) for a in a_lows]
            ts_ = [low + _cat_matmul(low, y, bd) for low, y in zip(lows, ys)]
            ys = [y - t - _cat_matmul(y, t, bd) for y, t in zip(ys, ts_)]
        bdr = bdr_ref[...]
        uws = [op[2] + _cat_matmul(y, op[2], bdr) for y, op in zip(ys, ops)]
        for j in group:
            _, _, _, q_dec, k_decs, lasts = ops[j]
            u, w = uws[j][:, :DN_WIDTH], uws[j][:, DN_WIDTH:]
            ws = jnp.dot(jnp.concatenate([w.astype(BF16), q_dec], axis=0), sbd_ref[...],
                         preferred_element_type=F32)
            vb = (u - ws[:CHUNK]).astype(BF16)
            o = ws[CHUNK:] + jnp.dot(qks[j], _block_diag(vb, bdk), preferred_element_type=F32)
            for h in heads:
                lo, hi = h * DN_HEAD_DIM, (h + 1) * DN_HEAD_DIM
                new = lasts[h] * state_ref[h] + lax.dot_general(k_decs[h], vb[:, lo:hi], TN_DIMS,
                                                                preferred_element_type=F32)
                state_ref[h] = new
                sbd_ref[lo:hi, lo:hi] = new.astype(BF16)
                oh = o[:, lo:hi]
                oh = oh * lax.rsqrt(jnp.mean(oh * oh, axis=-1, keepdims=True) + NORM_EPS) * onorm_g
                zz = z_ref[pl.ds(r0s[j], CHUNK), lo:hi].astype(F32)
                o_ref[pl.ds(r0s[j], CHUNK), lo:hi] = (oh * (zz * _sigmoid(zz))).astype(o_ref.dtype)
        return 0

    lax.fori_loop(0, ts // (CHUNK * GDN_GROUP), group_body, 0)


def _gated_deltanet(pc, pz, ps, alog_row, dt_row, onorm_row, batch, seq):
    ts = GDN_TS
    consts = _gdn_constants()
    const = lambda a: pl.BlockSpec(a.shape, lambda b, s: (0, 0))
    tile = lambda width: pl.BlockSpec((None, ts, width), lambda b, s: (b, s, 0))
    return pl.pallas_call(
        functools.partial(_gdn_kernel, ts=ts),
        grid=(batch, seq // ts),
        in_specs=[tile(3 * DN_WIDTH), tile(DN_WIDTH), tile(LANES), const(alog_row), const(dt_row),
                  const(onorm_row)] + [const(a) for a in consts],
        out_specs=tile(DN_WIDTH),
        out_shape=jax.ShapeDtypeStruct((batch, seq, DN_WIDTH), BF16),
        scratch_shapes=[pltpu.VMEM((DN_HEADS, DN_HEAD_DIM, DN_HEAD_DIM), F32),
                        pltpu.VMEM((DN_WIDTH, DN_WIDTH), BF16)],
        compiler_params=_params("parallel", "arbitrary"),
        name="gated_deltanet",
    )(pc, pz, ps, alog_row, dt_row, onorm_row, *consts)


def _top2_of4(vals):
    m1 = jnp.maximum(jnp.maximum(vals[0], vals[1]), jnp.maximum(vals[2], vals[3]))
    i1 = jnp.where(vals[0] == m1, 0, jnp.where(vals[1] == m1, 1, jnp.where(vals[2] == m1, 2, 3)))
    rest = [jnp.where(i1 == r, -jnp.inf, vals[r]) for r in range(4)]
    m2 = jnp.maximum(jnp.maximum(rest[0], rest[1]), jnp.maximum(rest[2], rest[3]))
    i2 = jnp.where(rest[0] == m2, 0, jnp.where(rest[1] == m2, 1, jnp.where(rest[2] == m2, 2, 3)))
    return m1, i1, m2, i2


def _pick4(idx, vals):
    return jnp.where(idx == 0, vals[0], jnp.where(idx == 1, vals[1], jnp.where(idx == 2, vals[2], vals[3])))


def _merge_kernel(oa_ref, ob_ref, ga_ref, gb_ref, x_ref, wa_ref, wb_ref, wo_ref, g_ref, b_ref,
                  wr_ref, rb_ref, h_ref, ri_ref, cnt_ref, carry_ref, *, tm, parts, alpha):
    t = pl.program_id(0)

    @pl.when(t == 0)
    def _():
        carry_ref[...] = jnp.zeros_like(carry_ref)

    pm = tm // parts
    subs = [slice(p * pm, (p + 1) * pm) for p in range(parts)]
    each = lambda f, *lists: [f(*args) for args in zip(*lists)]
    ya = [jnp.dot(oa_ref[r, :], wa_ref[...], preferred_element_type=F32) for r in subs]
    yb = [jnp.dot(ob_ref[r, :], wb_ref[...], preferred_element_type=F32) for r in subs]
    merged = [(_sigmoid(ga_ref[r, :].astype(F32)) * a + _sigmoid(gb_ref[r, :].astype(F32)) * b).astype(BF16)
              for r, a, b in zip(subs, ya, yb)]
    mix = [jnp.dot(m, wo_ref[...], preferred_element_type=F32) for m in merged]
    hs = [_layer_norm(alpha * x_ref[r, :] + mx, g_ref[...], b_ref[...]) for r, mx in zip(subs, mix)]
    for r, h in zip(subs, hs):
        h_ref[r, :D_MODEL] = h

    logits = [jnp.dot(h.astype(BF16), wr_ref[...], preferred_element_type=F32) for h in hs]
    lts = [lg.T[:N_EXPERTS, :] for lg in logits]
    exs = [jnp.exp(lt - jnp.max(lt, axis=0, keepdims=True)) for lt in lts]
    probs = [ex / jnp.sum(ex, axis=0, keepdims=True) for ex in exs]
    sels = [p + rb_ref[...] for p in probs]
    sel_rows = [[sel[e:e + 1, :] for e in range(N_EXPERTS)] for sel in sels]
    prob_rows = [[p[e:e + 1, :] for e in range(N_EXPERTS)] for p in probs]
    groups = range(N_GROUPS)
    local = range(EXPERTS_PER_GROUP)
    scores = [[sum(_top2_of4(rows[g * EXPERTS_PER_GROUP:(g + 1) * EXPERTS_PER_GROUP])[0:3:2]) for g in groups]
              for rows in sel_rows]
    best = [jnp.maximum(jnp.maximum(sc[0], sc[1]), jnp.maximum(sc[2], sc[3])) for sc in scores]
    grps = [jnp.where(sc[0] == bs, 0, jnp.where(sc[1] == bs, 1, jnp.where(sc[2] == bs, 2, 3)))
            for sc, bs in zip(scores, best)]
    in_grp = [[_pick4(grp, [rows[g * EXPERTS_PER_GROUP + r] for g in groups]) for r in local]
              for grp, rows in zip(grps, sel_rows)]
    p_grp = [[_pick4(grp, [rows[g * EXPERTS_PER_GROUP + r] for g in groups]) for r in local]
             for grp, rows in zip(grps, prob_rows)]
    tops = [_top2_of4(ig) for ig in in_grp]
    l0s = [tp[1] for tp in tops]
    l1s = [tp[3] for tp in tops]
    p0s = each(_pick4, l0s, p_grp)
    p1s = each(_pick4, l1s, p_grp)
    w0s = [p0 / (p0 + p1) for p0, p1 in zip(p0s, p1s)]
    w1s = [p1 / (p0 + p1) for p0, p1 in zip(p0s, p1s)]
    r128 = lax.broadcasted_iota(jnp.int32, (LANES, pm), 0)
    for r, l0, l1, w0, w1 in zip(subs, l0s, l1s, w0s, w1s):
        local_w = jnp.zeros((LANES, pm), F32)
        for e in local:
            local_w = jnp.where(r128 == e, jnp.where(l0 == e, w0, jnp.where(l1 == e, w1, 0.0)), local_w)
        h_ref[r, D_MODEL:] = local_w.T

    firsts = each(jnp.minimum, l0s, l1s)
    seconds = each(jnp.maximum, l0s, l1s)
    pair_ids = [jnp.where(f == 0, sd - 1, jnp.where(f == 1, sd + 1, PAIRS_PER_GROUP - 1))
                for f, sd in zip(firsts, seconds)]
    clss = [grp * PAIRS_PER_GROUP + pid for grp, pid in zip(grps, pair_ids)]
    cidx = lax.broadcasted_iota(jnp.int32, (CLASS_ROWS, pm), 0)
    hits = [cidx == cls for cls in clss]
    onehots = [jnp.where(hit, 1.0, 0.0) for hit in hits]
    trow = lax.broadcasted_iota(jnp.int32, (pm, pm), 0)
    tcol = lax.broadcasted_iota(jnp.int32, (pm, pm), 1)
    earlier = jnp.where(trow < tcol, 1.0, 0.0).astype(BF16)
    inside = [jnp.dot(oh.astype(BF16), earlier, preferred_element_type=F32) for oh in onehots]
    totals = [jnp.sum(oh, axis=1, keepdims=True) for oh in onehots]
    carry = carry_ref[...]
    r8 = lax.broadcasted_iota(jnp.int32, (8, pm), 0)
    for r, cls, hit, ins, tot in zip(subs, clss, hits, inside, totals):
        rank = jnp.sum(jnp.where(hit, ins + carry, 0.0), axis=0, keepdims=True).astype(jnp.int32)
        ri_ref[:, r] = jnp.where(r8 == 0, cls, jnp.where(r8 == 1, rank, 0))
        carry = carry + tot
    carry_ref[...] = carry
    cnt_ref[...] = jnp.broadcast_to(carry, cnt_ref.shape).astype(jnp.int32)


def _merge(o_a, o_b, gates, x, w_br_a, w_br_b, w_o, ln_g, ln_b, w_router, router_bias, alpha):
    n = o_a.shape[0]
    tm = MERGE_TM
    full = lambda shape: pl.BlockSpec(shape, lambda t: (0,) * len(shape))
    return pl.pallas_call(
        functools.partial(_merge_kernel, tm=tm, parts=MERGE_PARTS, alpha=alpha),
        grid=(n // tm,),
        in_specs=[pl.BlockSpec((tm, SB_WIDTH), lambda t: (t, 0)),
                  pl.BlockSpec((tm, DN_WIDTH), lambda t: (t, 0)),
                  pl.BlockSpec((tm, D_MODEL), lambda t: (t, 0)),
                  pl.BlockSpec((tm, D_MODEL), lambda t: (t, 1)),
                  pl.BlockSpec((tm, D_MODEL), lambda t: (t, 0)),
                  full((SB_WIDTH, D_MODEL)), full((DN_WIDTH, D_MODEL)), full((D_MODEL, D_MODEL)),
                  full((1, D_MODEL)), full((1, D_MODEL)),
                  full((D_MODEL, LANES)), full((N_EXPERTS, 1))],
        out_specs=[pl.BlockSpec((tm, H_EXT), lambda t: (t, 0)),
                   pl.BlockSpec((8, tm), lambda t: (0, t)),
                   pl.BlockSpec((CLASS_ROWS, LANES), lambda t: (0, 0))],
        out_shape=[jax.ShapeDtypeStruct((n, H_EXT), F32),
                   jax.ShapeDtypeStruct((8, n), jnp.int32),
                   jax.ShapeDtypeStruct((CLASS_ROWS, LANES), jnp.int32)],
        scratch_shapes=[pltpu.VMEM((CLASS_ROWS, 1), F32)],
        compiler_params=_params("arbitrary"),
        name="merge_ln_route",
    )(o_a, o_b, gates, gates, x, w_br_a, w_br_b, w_o, ln_g, ln_b, w_router, router_bias)


def _group_ffn_kernel(blk_grp_ref, src_ref, dst_ref, used_ref, n_used_ref, h_hbm, wg_ref, wu_ref, wd_ref,
                      g_ref, b_ref, f_hbm, xbuf, obuf, gsem, ssem, *, rb, n_blk, alpha):
    del blk_grp_ref
    i = pl.program_id(0)
    slot = i % 2
    other = 1 - slot

    def gather_row(block, r, s):
        return pltpu.make_async_copy(h_hbm.at[pl.ds(src_ref[block * rb + r], 1), :],
                                     xbuf.at[s, pl.ds(r, 1), :], gsem.at[s])

    def scatter_row(block, r, s):
        return pltpu.make_async_copy(obuf.at[s, pl.ds(r, 1), :],
                                     f_hbm.at[pl.ds(dst_ref[(block + 2) * rb + r], 1), :], ssem.at[s])

    @pl.when(i == 0)
    def _():
        obuf[...] = jnp.zeros_like(obuf)

        def issue(r, _):
            gather_row(0, r, 0).start()
            scatter_row(-2, r, 1).start()
            return 0

        lax.fori_loop(0, rb, issue, 0)
        pltpu.make_async_copy(obuf.at[1], f_hbm.at[pl.ds(0, rb), :], ssem.at[1]).wait()

    pltpu.make_async_copy(h_hbm.at[pl.ds(0, rb), :], xbuf.at[slot], gsem.at[slot]).wait()

    @pl.when(i > 0)
    def _():
        pltpu.make_async_copy(obuf.at[slot], f_hbm.at[pl.ds(0, rb), :], ssem.at[slot]).wait()

    @pl.when(i < n_blk)
    def _():
        x = xbuf[slot, :, :D_MODEL].astype(BF16)
        w_cols = [xbuf[slot, :, D_MODEL + e:D_MODEL + e + 1] for e in range(EXPERTS_PER_GROUP)]

        def apply(e):
            gate = jnp.dot(x, wg_ref[e], preferred_element_type=F32)
            up = jnp.dot(x, wu_ref[e], preferred_element_type=F32)
            hid = (gate * _sigmoid(gate) * up).astype(BF16)
            return w_cols[e] * jnp.dot(hid, wd_ref[e], preferred_element_type=F32)

        part = rb // 2
        for k in range(EXPERTS_PER_GROUP):
            for e in range(k, min(k + 3, EXPERTS_PER_GROUP)):
                @pl.when((used_ref[i * EXPERTS_PER_GROUP + k] == e) & (n_used_ref[i] > k))
                def _():
                    if k < 2:
                        for r in range(k * part, (k + 1) * part):
                            gather_row(i + 1, r, other).start(priority=0)
                            scatter_row(i - 1, r, other).start(priority=1)
                    if k == 0:
                        obuf[slot] = apply(e)
                    else:
                        obuf[slot] += apply(e)

        obuf[slot] = _layer_norm(alpha * xbuf[slot, :, :D_MODEL] + obuf[slot], g_ref[...], b_ref[...])

    @pl.when(i == n_blk)
    def _():
        def issue(r, _):
            scatter_row(n_blk - 1, r, other).start()
            return 0

        lax.fori_loop(0, rb, issue, 0)
        pltpu.make_async_copy(obuf.at[other], f_hbm.at[pl.ds(0, rb), :], ssem.at[other]).wait()


def _group_ffn(h_ext, blk_grp, row_src, row_dst, used, n_used, w_gate, w_up, w_down, ln_g, ln_b, alpha, n_tok):
    rb = ROUTE_BLOCK
    n_blk = row_src.shape[0] // rb - 1
    wspec = lambda a, b: pl.BlockSpec((EXPERTS_PER_GROUP, a, b), lambda i, bg, rs, rd, us, nu: (bg[i], 0, 0))
    vec = pl.BlockSpec((1, D_MODEL), lambda i, bg, rs, rd, us, nu: (0, 0))
    return pl.pallas_call(
        functools.partial(_group_ffn_kernel, rb=rb, n_blk=n_blk, alpha=alpha),
        grid_spec=pltpu.PrefetchScalarGridSpec(
            num_scalar_prefetch=5,
            grid=(n_blk + 1,),
            in_specs=[pl.BlockSpec(memory_space=pl.ANY),
                      wspec(D_MODEL, D_EXPERT), wspec(D_MODEL, D_EXPERT), wspec(D_EXPERT, D_MODEL),
                      vec, vec],
            out_specs=pl.BlockSpec(memory_space=pl.ANY),
            scratch_shapes=[pltpu.VMEM((2, rb, H_EXT), F32), pltpu.VMEM((2, rb, D_MODEL), F32),
                            pltpu.SemaphoreType.DMA((2,)), pltpu.SemaphoreType.DMA((2,))]),
        out_shape=jax.ShapeDtypeStruct((n_tok + 2 * rb, D_MODEL), F32),
        compiler_params=_params("arbitrary"),
        name="group_ffn",
    )(blk_grp, row_src, row_dst, used, n_used, h_ext, w_gate, w_up, w_down, ln_g, ln_b)


def _route_rows(ri, cnt, n_tok):
    rb = ROUTE_BLOCK
    n_cls = N_GROUPS * PAIRS_PER_GROUP
    c_count = cnt[:n_cls, 0].reshape(N_GROUPS, PAIRS_PER_GROUP)
    g_count = jnp.sum(c_count, axis=1)
    g_padded = (g_count + rb - 1) // rb * rb
    g_end = jnp.cumsum(g_padded)
    c_start = ((g_end - g_padded)[:, None] + jnp.cumsum(c_count, axis=1) - c_count).reshape(n_cls)
    c_count = c_count.reshape(n_cls)
    dest = c_start[ri[0]] + ri[1]
    p_rows = n_tok + N_GROUPS * rb
    row_tok = jnp.full((p_rows,), -1, jnp.int32).at[dest].set(jnp.arange(n_tok, dtype=jnp.int32))
    spare = n_tok + jnp.arange(-2 * rb, p_rows, dtype=jnp.int32) % (2 * rb)
    row_src = jnp.concatenate([jnp.maximum(row_tok, 0), jnp.zeros((rb,), jnp.int32)])
    row_dst = jnp.concatenate([jnp.full((2 * rb,), -1, jnp.int32), row_tok])
    row_dst = jnp.where(row_dst < 0, spare, row_dst)
    blk_start = jnp.arange(p_rows // rb + 1, dtype=jnp.int32) * rb
    blk_grp = jnp.minimum(jnp.searchsorted(g_end, blk_start, side="right"), N_GROUPS - 1).astype(jnp.int32)
    pairs = [(a, b) for a in range(EXPERTS_PER_GROUP) for b in range(a + 1, EXPERTS_PER_GROUP)]
    member = jnp.array([[e in pairs[c % PAIRS_PER_GROUP] for e in range(EXPERTS_PER_GROUP)]
                        for c in range(n_cls)])
    overlap = ((c_start[None, :] < blk_start[:, None] + rb) & ((c_start + c_count)[None, :] > blk_start[:, None])
               & (c_count[None, :] > 0))
    need = jnp.any(overlap[:, :, None] & member[None, :, :], axis=1)
    used = jnp.argsort(~need, axis=1, stable=True).astype(jnp.int32)
    n_used = jnp.maximum(jnp.sum(need, axis=1), 2).astype(jnp.int32)
    return row_src, row_dst, blk_grp, used.reshape(-1), n_used


def _lane_row(values, offset):
    return jnp.zeros((1, LANES), F32).at[0, offset:offset + values.shape[0]].set(values.astype(F32))


def kernel(x, w_in, conv_w, a_log, dt_bias, onorm_g, w_br_a, w_br_b, w_o, ln1_g, ln1_b, w_router,
           router_bias, w_gate, w_up, w_down, ln2_g, ln2_b):
    batch, seq, d = x.shape
    depth = w_in.shape[0]
    n_tok = batch * seq
    alpha = (2 * depth) ** 0.25
    a_end = 3 * SB_WIDTH
    b_end = a_end + 4 * DN_WIDTH
    s_end = b_end + 2 * DN_HEADS
    w_a = w_in[:, :, :a_end].astype(BF16)
    w_b = w_in[:, :, a_end:b_end].astype(BF16)
    w_s = jnp.pad(w_in[:, :, b_end:s_end], ((0, 0), (0, 0), (0, LANES - 2 * DN_HEADS))).astype(BF16)
    w_g = w_in[:, :, s_end:].astype(BF16)
    w_r = jnp.pad(w_router, ((0, 0), (0, LANES - N_EXPERTS))).astype(BF16)
    r_bias = router_bias.astype(F32).reshape(N_EXPERTS, 1)
    w_br_a, w_br_b, w_o = w_br_a.astype(BF16), w_br_b.astype(BF16), w_o.astype(BF16)
    w_gate, w_up, w_down = w_gate.astype(BF16), w_up.astype(BF16), w_down.astype(BF16)

    xf = x.reshape(n_tok, d)
    for l in range(depth):
        pa, pc, pz, ps, gates = _project(xf, n_tok, w_a[l], w_b[l], w_s[l], w_g[l], conv_w[l].astype(F32), seq)
        pa = pa.reshape(batch, seq, a_end)
        pc = pc.reshape(batch, seq, 3 * DN_WIDTH)
        pz = pz.reshape(batch, seq, DN_WIDTH)
        ps = ps.reshape(batch, seq, LANES)
        o_a = _stick_breaking(pa, batch, seq).reshape(n_tok, SB_WIDTH)
        o_b = _gated_deltanet(pc, pz, ps, _lane_row(a_log[l], DN_HEADS), _lane_row(dt_bias[l], DN_HEADS),
                              onorm_g[l].astype(F32).reshape(1, DN_HEAD_DIM), batch, seq).reshape(n_tok, DN_WIDTH)
        h_ext, ri, cnt = _merge(o_a, o_b, gates, xf, w_br_a[l], w_br_b[l], w_o[l],
                                ln1_g[l].reshape(1, d), ln1_b[l].reshape(1, d), w_r, r_bias, alpha)
        row_src, row_dst, blk_grp, used, n_used = _route_rows(ri, cnt, n_tok)
        xf = _group_ffn(h_ext, blk_grp, row_src, row_dst, used, n_used, w_gate[l], w_up[l], w_down[l],
                        ln2_g[l].reshape(1, d), ln2_b[l].reshape(1, d), alpha, n_tok)
    return xf[:n_tok].reshape(batch, seq, d)
```

```python
import functools

import jax
import jax.numpy as jnp
from jax import lax
from jax.experimental import pallas as pl
from jax.experimental.pallas import tpu as pltpu

D_MODEL = 1024
SB_HEADS = 8
SB_HEAD_DIM = 64
SB_WIDTH = SB_HEADS * SB_HEAD_DIM
DN_HEADS = 4
DN_HEAD_DIM = 128
DN_WIDTH = DN_HEADS * DN_HEAD_DIM
CONV_K = 4
CHUNK = 64
N_EXPERTS = 16
N_GROUPS = 4
EXPERTS_PER_GROUP = N_EXPERTS // N_GROUPS
PAIRS_PER_GROUP = EXPERTS_PER_GROUP * (EXPERTS_PER_GROUP - 1) // 2
CLASS_ROWS = 32
D_EXPERT = 512
LN_EPS = 1e-5
NORM_EPS = 1e-6

LANES = 128
H_EXT = D_MODEL + LANES
VMEM_LIMIT = 48 * 1024 * 1024

PROJ_TM = 512
SB_TQ = 128
SB_TK = 256
SB_SUB = 16
GDN_TS = 512
GDN_GROUP = 8
MERGE_TM = 1024
MERGE_PARTS = 4
ROUTE_BLOCK = 256
SB_SKIP_LOG = -60.0

BF16 = jnp.bfloat16
F32 = jnp.float32
NT_DIMS = (((1,), (1,)), ((), ()))
TN_DIMS = (((0,), (0,)), ((), ()))


def _params(*semantics):
    return pltpu.CompilerParams(dimension_semantics=semantics, vmem_limit_bytes=VMEM_LIMIT)


def _sigmoid(x):
    return 1.0 / (1.0 + jnp.exp(-x))


def _softplus(x):
    return jnp.maximum(x, 0.0) + jnp.log(1.0 + jnp.exp(-jnp.abs(x)))


def _layer_norm(y, g, b):
    mu = jnp.mean(y, axis=-1, keepdims=True)
    yc = y - mu
    var = jnp.mean(yc * yc, axis=-1, keepdims=True)
    return yc * lax.rsqrt(var + LN_EPS) * g + b


def _proj_kernel(x_ref, wa_ref, wb_ref, ws_ref, wg_ref, cw_ref, oa_ref, oc_ref, oz_ref, os_ref, og_ref,
                 ubuf, *, tm, tiles_per_seq):
    i = pl.program_id(0)
    tail = 8
    qkv = 3 * DN_WIDTH

    @pl.when(i == 0)
    def _():
        ubuf[tm:tm + tail, :] = jnp.zeros((tail, qkv), F32)

    x = x_ref[...].astype(BF16)
    oa_ref[...] = jnp.dot(x, wa_ref[...], preferred_element_type=F32).astype(oa_ref.dtype)
    pb = jnp.dot(x, wb_ref[...], preferred_element_type=F32)
    oz_ref[...] = pb[:, qkv:].astype(oz_ref.dtype)
    prev = ubuf[tm:tm + tail, :]
    ubuf[0:tail, :] = jnp.where(i % tiles_per_seq == 0, jnp.zeros_like(prev), prev)
    ubuf[tail:tail + tm, :] = pb[:, :qkv]
    os_ref[...] = jnp.dot(x, ws_ref[...], preferred_element_type=F32)
    gate_cols = wg_ref.shape[1]
    pieces = 4
    step = gate_cols // pieces
    lane_tiles = qkv // LANES
    for p in range(pieces):
        for c in range(p * lane_tiles // pieces, (p + 1) * lane_tiles // pieces):
            lo, hi = c * LANES, (c + 1) * LANES
            u = ubuf[:, lo:hi]
            conv = cw_ref[CONV_K - 1:CONV_K, lo:hi] * u[tail:, :]
            for j in range(CONV_K - 1):
                conv = conv + cw_ref[j:j + 1, lo:hi] * pltpu.roll(u, CONV_K - 1 - j, axis=0)[tail:, :]
            oc_ref[:, lo:hi] = (conv * _sigmoid(conv)).astype(oc_ref.dtype)
        og_ref[:, p * step:(p + 1) * step] = jnp.dot(
            x, wg_ref[:, p * step:(p + 1) * step], preferred_element_type=F32).astype(og_ref.dtype)


def _project(x, m, w_a, w_b, w_s, w_g, conv_w, seq):
    k = x.shape[1]
    tm = PROJ_TM
    qkv = 3 * DN_WIDTH
    widths = (w_a.shape[1], qkv, w_b.shape[1] - qkv, w_s.shape[1], w_g.shape[1])
    dtypes = (BF16, BF16, BF16, F32, BF16)
    const = lambda a: pl.BlockSpec(a.shape, lambda i: (0, 0))
    return pl.pallas_call(
        functools.partial(_proj_kernel, tm=tm, tiles_per_seq=seq // tm),
        grid=(m // tm,),
        in_specs=[pl.BlockSpec((tm, k), lambda i: (i, 0)), const(w_a), const(w_b), const(w_s), const(w_g),
                  const(conv_w)],
        out_specs=[pl.BlockSpec((tm, n), lambda i: (i, 0)) for n in widths],
        out_shape=[jax.ShapeDtypeStruct((m, n), dt) for n, dt in zip(widths, dtypes)],
        scratch_shapes=[pltpu.VMEM((tm + 8, qkv), F32)],
        compiler_params=_params("arbitrary"),
        name="proj",
    )(x, w_a, w_b, w_s, w_g, conv_w)


def _sb_kernel(q_ref, k_ref, v_ref, later_ref, o_ref, acc_ref, c_ref, *, tq, tk, sub):
    i = pl.program_id(2)
    lane = lax.broadcasted_iota(jnp.int32, (1, LANES), 1)
    head_a = lane < SB_HEAD_DIM
    row1 = lax.broadcasted_iota(jnp.int32, (tq, 1), 0)
    col = lax.broadcasted_iota(jnp.int32, (tq, tk), 1)
    col_b = col.astype(BF16)
    later = later_ref[...]
    tiles = range(sub)
    pairs = [(s, h) for s in tiles for h in range(2)]
    q_ends = [(i * sub + s + 1) * tq for s in tiles]
    q_heads = []
    for s in tiles:
        q = q_ref[s * tq:(s + 1) * tq, :] * jnp.asarray(SB_HEAD_DIM ** -0.5, q_ref.dtype)
        zero = jnp.zeros_like(q)
        q_heads.append((jnp.where(head_a, q, zero), jnp.where(head_a, zero, q)))

    def walk(b, first):
        nominal = [qe - (b + 1) * tk for qe in q_ends]
        starts = [pl.multiple_of(jnp.maximum(nm, 0), tq) for nm in nominal]
        ks = [k_ref[pl.ds(st, tk), :] for st in starts]
        vs = [v_ref[pl.ds(st, tk), :] for st in starts]
        bounds = [jnp.minimum(qe - tq + row1 - st, nm + tk - st) for st, qe, nm in zip(starts, q_ends, nominal)]
        valids = [col < bd for bd in bounds]
        valids_b = [col_b < bd.astype(BF16) for bd in bounds]
        zs = [lax.dot_general(q_heads[s][h], ks[s], NT_DIMS, preferred_element_type=F32) for s, h in pairs]
        zbs = [z.astype(BF16) for z in zs]
        log_betas = [jnp.minimum(zb, 0) - jnp.log(1 + jnp.exp(-jnp.abs(zb))) for zb in zbs]
        log_keeps = [jnp.where(valids_b[s], lb - zb, 0) for (s, _), lb, zb in zip(pairs, log_betas, zbs)]
        sums = [jnp.dot(lk, later, preferred_element_type=F32) for lk in log_keeps]
        weights = [jnp.where(valids[s], jnp.exp(lb.astype(F32) + sm), 0.0).astype(BF16)
                   for (s, _), lb, sm in zip(pairs, log_betas, sums)]
        totals = [sm[:, 0:1] + lk[:, 0:1].astype(F32) for sm, lk in zip(sums, log_keeps)]
        pvs = [jnp.dot(a, vs[s], preferred_element_type=F32) for (s, _), a in zip(pairs, weights)]
        cmax = [None] * sub
        for (s, h), pv, tot in zip(pairs, pvs, totals):
            if first:
                acc_ref[s, h] = pv
                c = jnp.broadcast_to(tot, (tq, LANES))
            else:
                c = c_ref[s, h]
                acc_ref[s, h] += jnp.exp(c) * pv
                c = c + tot
            c_ref[s, h] = c
            m = jnp.max(c)
            cmax[s] = m if cmax[s] is None else jnp.maximum(cmax[s], m)
        return cmax

    def unfinished(carry):
        b = carry[0]
        need = [(qe - b * tk > 0) & (cm > SB_SKIP_LOG) for qe, cm in zip(q_ends, carry[1:])]
        return functools.reduce(jnp.logical_or, need)

    lax.while_loop(unfinished, lambda carry: (carry[0] + 1, *walk(carry[0], False)),
                   (jnp.int32(1), *walk(0, True)))
    for s in tiles:
        o_ref[s * tq:(s + 1) * tq, :] = jnp.where(head_a, acc_ref[s, 0], acc_ref[s, 1]).astype(o_ref.dtype)


def _stick_breaking(qkv, batch, seq):
    pairs = SB_WIDTH // LANES
    tq, tk, sub = SB_TQ, SB_TK, SB_SUB
    rows = tq * sub
    later = (jnp.arange(tk)[:, None] > jnp.arange(tk)[None, :]).astype(BF16)
    return pl.pallas_call(
        functools.partial(_sb_kernel, tq=tq, tk=tk, sub=sub),
        grid=(batch, pairs, seq // rows),
        in_specs=[pl.BlockSpec((None, rows, LANES), lambda b, p, i: (b, i, p)),
                  pl.BlockSpec((None, seq, LANES), lambda b, p, i: (b, 0, pairs + p)),
                  pl.BlockSpec((None, seq, LANES), lambda b, p, i: (b, 0, 2 * pairs + p)),
                  pl.BlockSpec((tk, tk), lambda b, p, i: (0, 0))],
        out_specs=pl.BlockSpec((None, rows, LANES), lambda b, p, i: (b, i, p)),
        out_shape=jax.ShapeDtypeStruct((batch, seq, SB_WIDTH), BF16),
        scratch_shapes=[pltpu.VMEM((sub, 2, tq, LANES), F32), pltpu.VMEM((sub, 2, tq, LANES), F32)],
        compiler_params=_params("parallel", "parallel", "parallel"),
        name="stick_breaking",
    )(qkv, qkv, qkv, later)


CAT = DN_HEADS * CHUNK


def _split3(x):
    hi = x.astype(BF16)
    r = x - hi.astype(F32)
    mid = r.astype(BF16)
    return hi, mid, (r - mid.astype(F32)).astype(BF16)


def _block_diag(x, mask):
    return jnp.concatenate([x] * DN_HEADS, axis=0) * mask


def _cat_matmul(l, r, mask):
    return jnp.dot(l.astype(BF16), _block_diag(r.astype(BF16), mask), preferred_element_type=F32)


def _by_head(head, vals):
    return jnp.where(head == 0, vals[0], jnp.where(head == 1, vals[1], jnp.where(head == 2, vals[2], vals[3])))


def _gdn_constants():
    i = jnp.arange(CHUNK)
    tri = (i[:, None] >= i[None, :]).astype(BF16)
    row_head = jnp.arange(CAT)[:, None] // CHUNK
    wide_head = jnp.arange(DN_WIDTH) // DN_HEAD_DIM
    return (jnp.concatenate([tri] * 3, axis=1),
            (row_head == jnp.arange(CAT)[None, :] // CHUNK).astype(BF16),
            (row_head == wide_head[None, :]).astype(BF16),
            (row_head == jnp.concatenate([wide_head, wide_head])[None, :]).astype(BF16))


def _gdn_kernel(qkv_ref, z_ref, s_ref, alog_ref, dt_ref, og_ref, tri3_ref, bd_ref, bdk_ref,
                bdr_ref, o_ref, state_ref, sbd_ref, *, ts):
    @pl.when(pl.program_id(1) == 0)
    def _():
        state_ref[...] = jnp.zeros_like(state_ref)
        sbd_ref[...] = jnp.zeros_like(sbd_ref)

    rows = lax.broadcasted_iota(jnp.int32, (CHUNK, CAT), 0)
    lane = lax.broadcasted_iota(jnp.int32, (CHUNK, CAT), 1)
    cols = lane % CHUNK
    head = lane // CHUNK
    incl = rows >= cols
    strict = rows > cols
    upper = rows <= cols
    pair = rows // 2 == cols // 2
    lower_blocks = []
    blk = 2
    while blk < CHUNK:
        lower_blocks.append((rows // (2 * blk) == cols // (2 * blk)) & (rows // blk != cols // blk))
        blk *= 2
    neg_a = -jnp.exp(alog_ref[...])
    dt_bias = dt_ref[...]
    onorm_g = og_ref[...]
    heads = range(DN_HEADS)

    def operands(r0, beta_all, gcol):
        qs, ks, q_decs, k_decs, rhs_v, rhs_k, lasts = [], [], [], [], [], [], []
        for h in heads:
            lo, hi = h * DN_HEAD_DIM, (h + 1) * DN_HEAD_DIM
            q = qkv_ref[pl.ds(r0, CHUNK), lo:hi].astype(F32)
            k = qkv_ref[pl.ds(r0, CHUNK), DN_WIDTH + lo:DN_WIDTH + hi].astype(F32)
            v = qkv_ref[pl.ds(r0, CHUNK), 2 * DN_WIDTH + lo:2 * DN_WIDTH + hi].astype(F32)
            q = q * lax.rsqrt(jnp.sum(q * q, axis=-1, keepdims=True) + NORM_EPS) * (DN_HEAD_DIM ** -0.5)
            k = k * lax.rsqrt(jnp.sum(k * k, axis=-1, keepdims=True) + NORM_EPS)
            beta = beta_all[:, h:h + 1]
            gc = gcol[:, h * CHUNK:h * CHUNK + 1]
            glast = gc[CHUNK - 1:CHUNK, :]
            eg = jnp.exp(gc)
            qs.append(q)
            ks.append(k)
            q_decs.append(q * eg)
            k_decs.append((k * jnp.exp(glast - gc)).astype(BF16))
            rhs_v.append(beta * v)
            rhs_k.append(beta * eg * k)
            lasts.append(jnp.exp(glast))
        return (jnp.concatenate(ks, axis=1).astype(BF16), jnp.concatenate(qs, axis=1).astype(BF16),
                jnp.concatenate(rhs_v + rhs_k, axis=1), jnp.concatenate(q_decs, axis=1).astype(BF16),
                k_decs, lasts)

    def group_body(gi, _):
        group = range(GDN_GROUP)
        r0s = [pl.multiple_of((gi * GDN_GROUP + j) * CHUNK, CHUNK) for j in group]
        sms = [s_ref[pl.ds(r0, CHUNK), :] for r0 in r0s]
        beta_alls = [_sigmoid(sm) for sm in sms]
        g_alls = [neg_a * _softplus(sm + dt_bias) for sm in sms]
        beta_cats = [_by_head(head, [b[:, h:h + 1] for h in heads]) for b in beta_alls]
        g_cats = [_by_head(head, [g[:, DN_HEADS + h:DN_HEADS + h + 1] for h in heads]) for g in g_alls]
        gcols = [jnp.dot(tri3_ref[...], jnp.concatenate(_split3(g), axis=0), preferred_element_type=F32)
                 for g in g_cats]
        grows = [jnp.sum(jnp.where(upper, g, 0.0), axis=0, keepdims=True) for g in g_cats]
        gammas = [jnp.exp(jnp.where(incl, gc - gr, -jnp.inf)) for gc, gr in zip(gcols, grows)]
        ops = [operands(r0, b, gc) for r0, b, gc in zip(r0s, beta_alls, gcols)]
        bdk = bdk_ref[...]
        kqs = [lax.dot_general(jnp.concatenate([kc, qc], axis=0), _block_diag(kc, bdk), NT_DIMS,
                               preferred_element_type=F32) for kc, qc, _, _, _, _ in ops]
        qks = [(kq[CHUNK:] * gm).astype(BF16) for kq, gm in zip(kqs, gammas)]
        a_lows = [jnp.where(strict, b * kq[:CHUNK] * gm, 0.0) for b, kq, gm in zip(beta_cats, kqs, gammas)]
        bd = bd_ref[...]
        ys = [-jnp.where(pair, a, 0.0) for a in a_lows]
        for lower in lower_blocks:
            lows = [jnp.where(lower, a, 0.0) for a in a_lows]
            ts_ = [low + _cat_matmul(low, y, bd) for low, y in zip(lows, ys)]
            ys = [y - t - _cat_matmul(y, t, bd) for y, t in zip(ys, ts_)]
        bdr = bdr_ref[...]
        uws = [op[2] + _cat_matmul(y, op[2], bdr) for y, op in zip(ys, ops)]
        for j in group:
            _, _, _, q_dec, k_decs, lasts = ops[j]
            u, w = uws[j][:, :DN_WIDTH], uws[j][:, DN_WIDTH:]
            ws = jnp.dot(jnp.concatenate([w.astype(BF16), q_dec], axis=0), sbd_ref[...],
                         preferred_element_type=F32)
            vb = (u - ws[:CHUNK]).astype(BF16)
            o = ws[CHUNK:] + jnp.dot(qks[j], _block_diag(vb, bdk), preferred_element_type=F32)
            for h in heads:
                lo, hi = h * DN_HEAD_DIM, (h + 1) * DN_HEAD_DIM
                new = lasts[h] * state_ref[h] + lax.dot_general(k_decs[h], vb[:, lo:hi], TN_DIMS,
                                                                preferred_element_type=F32)
                state_ref[h] = new
                sbd_ref[lo:hi, lo:hi] = new.astype(BF16)
                oh = o[:, lo:hi]
                oh = oh * lax.rsqrt(jnp.mean(oh * oh, axis=-1, keepdims=True) + NORM_EPS) * onorm_g
                zz = z_ref[pl.ds(r0s[j], CHUNK), lo:hi].astype(F32)
                o_ref[pl.ds(r0s[j], CHUNK), lo:hi] = (oh * (zz * _sigmoid(zz))).astype(o_ref.dtype)
        return 0

    lax.fori_loop(0, ts // (CHUNK * GDN_GROUP), group_body, 0)


def _gated_deltanet(pc, pz, ps, alog_row, dt_row, onorm_row, batch, seq):
    ts = GDN_TS
    consts = _gdn_constants()
    const = lambda a: pl.BlockSpec(a.shape, lambda b, s: (0, 0))
    tile = lambda width: pl.BlockSpec((None, ts, width), lambda b, s: (b, s, 0))
    return pl.pallas_call(
        functools.partial(_gdn_kernel, ts=ts),
        grid=(batch, seq // ts),
        in_specs=[tile(3 * DN_WIDTH), tile(DN_WIDTH), tile(LANES), const(alog_row), const(dt_row),
                  const(onorm_row)] + [const(a) for a in consts],
        out_specs=tile(DN_WIDTH),
        out_shape=jax.ShapeDtypeStruct((batch, seq, DN_WIDTH), BF16),
        scratch_shapes=[pltpu.VMEM((DN_HEADS, DN_HEAD_DIM, DN_HEAD_DIM), F32),
                        pltpu.VMEM((DN_WIDTH, DN_WIDTH), BF16)],
        compiler_params=_params("parallel", "arbitrary"),
        name="gated_deltanet",
    )(pc, pz, ps, alog_row, dt_row, onorm_row, *consts)


def _top2_of4(vals):
    m1 = jnp.maximum(jnp.maximum(vals[0], vals[1]), jnp.maximum(vals[2], vals[3]))
    i1 = jnp.where(vals[0] == m1, 0, jnp.where(vals[1] == m1, 1, jnp.where(vals[2] == m1, 2, 3)))
    rest = [jnp.where(i1 == r, -jnp.inf, vals[r]) for r in range(4)]
    m2 = jnp.maximum(jnp.maximum(rest[0], rest[1]), jnp.maximum(rest[2], rest[3]))
    i2 = jnp.where(rest[0] == m2, 0, jnp.where(rest[1] == m2, 1, jnp.where(rest[2] == m2, 2, 3)))
    return m1, i1, m2, i2


def _pick4(idx, vals):
    return jnp.where(idx == 0, vals[0], jnp.where(idx == 1, vals[1], jnp.where(idx == 2, vals[2], vals[3])))


def _merge_kernel(oa_ref, ob_ref, ga_ref, gb_ref, x_ref, wa_ref, wb_ref, wo_ref, g_ref, b_ref,
                  wr_ref, rb_ref, h_ref, ri_ref, cnt_ref, carry_ref, *, tm, parts, alpha):
    t = pl.program_id(0)

    @pl.when(t == 0)
    def _():
        carry_ref[...] = jnp.zeros_like(carry_ref)

    pm = tm // parts
    subs = [slice(p * pm, (p + 1) * pm) for p in range(parts)]
    each = lambda f, *lists: [f(*args) for args in zip(*lists)]
    ya = [jnp.dot(oa_ref[r, :], wa_ref[...], preferred_element_type=F32) for r in subs]
    yb = [jnp.dot(ob_ref[r, :], wb_ref[...], preferred_element_type=F32) for r in subs]
    merged = [(_sigmoid(ga_ref[r, :].astype(F32)) * a + _sigmoid(gb_ref[r, :].astype(F32)) * b).astype(BF16)
              for r, a, b in zip(subs, ya, yb)]
    mix = [jnp.dot(m, wo_ref[...], preferred_element_type=F32) for m in merged]
    hs = [_layer_norm(alpha * x_ref[r, :] + mx, g_ref[...], b_ref[...]) for r, mx in zip(subs, mix)]
    for r, h in zip(subs, hs):
        h_ref[r, :D_MODEL] = h

    logits = [jnp.dot(h.astype(BF16), wr_ref[...], preferred_element_type=F32) for h in hs]
    lts = [lg.T[:N_EXPERTS, :] for lg in logits]
    exs = [jnp.exp(lt - jnp.max(lt, axis=0, keepdims=True)) for lt in lts]
    probs = [ex / jnp.sum(ex, axis=0, keepdims=True) for ex in exs]
    sels = [p + rb_ref[...] for p in probs]
    sel_rows = [[sel[e:e + 1, :] for e in range(N_EXPERTS)] for sel in sels]
    prob_rows = [[p[e:e + 1, :] for e in range(N_EXPERTS)] for p in probs]
    groups = range(N_GROUPS)
    local = range(EXPERTS_PER_GROUP)
    scores = [[sum(_top2_of4(rows[g * EXPERTS_PER_GROUP:(g + 1) * EXPERTS_PER_GROUP])[0:3:2]) for g in groups]
              for rows in sel_rows]
    best = [jnp.maximum(jnp.maximum(sc[0], sc[1]), jnp.maximum(sc[2], sc[3])) for sc in scores]
    grps = [jnp.where(sc[0] == bs, 0, jnp.where(sc[1] == bs, 1, jnp.where(sc[2] == bs, 2, 3)))
            for sc, bs in zip(scores, best)]
    in_grp = [[_pick4(grp, [rows[g * EXPERTS_PER_GROUP + r] for g in groups]) for r in local]
              for grp, rows in zip(grps, sel_rows)]
    p_grp = [[_pick4(grp, [rows[g * EXPERTS_PER_GROUP + r] for g in groups]) for r in local]
             for grp, rows in zip(grps, prob_rows)]
    tops = [_top2_of4(ig) for ig in in_grp]
    l0s = [tp[1] for tp in tops]
    l1s = [tp[3] for tp in tops]
    p0s = each(_pick4, l0s, p_grp)
    p1s = each(_pick4, l1s, p_grp)
    w0s = [p0 / (p0 + p1) for p0, p1 in zip(p0s, p1s)]
    w1s = [p1 / (p0 + p1) for p0, p1 in zip(p0s, p1s)]
    r128 = lax.broadcasted_iota(jnp.int32, (LANES, pm), 0)
    for r, l0, l1, w0, w1 in zip(subs, l0s, l1s, w0s, w1s):
        local_w = jnp.zeros((LANES, pm), F32)
        for e in local:
            local_w = jnp.where(r128 == e, jnp.where(l0 == e, w0, jnp.where(l1 == e, w1, 0.0)), local_w)
        h_ref[r, D_MODEL:] = local_w.T

    firsts = each(jnp.minimum, l0s, l1s)
    seconds = each(jnp.maximum, l0s, l1s)
    pair_ids = [jnp.where(f == 0, sd - 1, jnp.where(f == 1, sd + 1, PAIRS_PER_GROUP - 1))
                for f, sd in zip(firsts, seconds)]
    clss = [grp * PAIRS_PER_GROUP + pid for grp, pid in zip(grps, pair_ids)]
    cidx = lax.broadcasted_iota(jnp.int32, (CLASS_ROWS, pm), 0)
    hits = [cidx == cls for cls in clss]
    onehots = [jnp.where(hit, 1.0, 0.0) for hit in hits]
    trow = lax.broadcasted_iota(jnp.int32, (pm, pm), 0)
    tcol = lax.broadcasted_iota(jnp.int32, (pm, pm), 1)
    earlier = jnp.where(trow < tcol, 1.0, 0.0).astype(BF16)
    inside = [jnp.dot(oh.astype(BF16), earlier, preferred_element_type=F32) for oh in onehots]
    totals = [jnp.sum(oh, axis=1, keepdims=True) for oh in onehots]
    carry = carry_ref[...]
    r8 = lax.broadcasted_iota(jnp.int32, (8, pm), 0)
    for r, cls, hit, ins, tot in zip(subs, clss, hits, inside, totals):
        rank = jnp.sum(jnp.where(hit, ins + carry, 0.0), axis=0, keepdims=True).astype(jnp.int32)
        ri_ref[:, r] = jnp.where(r8 == 0, cls, jnp.where(r8 == 1, rank, 0))
        carry = carry + tot
    carry_ref[...] = carry
    cnt_ref[...] = jnp.broadcast_to(carry, cnt_ref.shape).astype(jnp.int32)


def _merge(o_a, o_b, gates, x, w_br_a, w_br_b, w_o, ln_g, ln_b, w_router, router_bias, alpha):
    n = o_a.shape[0]
    tm = MERGE_TM
    full = lambda shape: pl.BlockSpec(shape, lambda t: (0,) * len(shape))
    return pl.pallas_call(
        functools.partial(_merge_kernel, tm=tm, parts=MERGE_PARTS, alpha=alpha),
        grid=(n // tm,),
        in_specs=[pl.BlockSpec((tm, SB_WIDTH), lambda t: (t, 0)),
                  pl.BlockSpec((tm, DN_WIDTH), lambda t: (t, 0)),
                  pl.BlockSpec((tm, D_MODEL), lambda t: (t, 0)),
                  pl.BlockSpec((tm, D_MODEL), lambda t: (t, 1)),
                  pl.BlockSpec((tm, D_MODEL), lambda t: (t, 0)),
                  full((SB_WIDTH, D_MODEL)), full((DN_WIDTH, D_MODEL)), full((D_MODEL, D_MODEL)),
                  full((1, D_MODEL)), full((1, D_MODEL)),
                  full((D_MODEL, LANES)), full((N_EXPERTS, 1))],
        out_specs=[pl.BlockSpec((tm, H_EXT), lambda t: (t, 0)),
                   pl.BlockSpec((8, tm), lambda t: (0, t)),
                   pl.BlockSpec((CLASS_ROWS, LANES), lambda t: (0, 0))],
        out_shape=[jax.ShapeDtypeStruct((n, H_EXT), F32),
                   jax.ShapeDtypeStruct((8, n), jnp.int32),
                   jax.ShapeDtypeStruct((CLASS_ROWS, LANES), jnp.int32)],
        scratch_shapes=[pltpu.VMEM((CLASS_ROWS, 1), F32)],
        compiler_params=_params("arbitrary"),
        name="merge_ln_route",
    )(o_a, o_b, gates, gates, x, w_br_a, w_br_b, w_o, ln_g, ln_b, w_router, router_bias)


def _group_ffn_kernel(blk_grp_ref, src_ref, dst_ref, used_ref, n_used_ref, h_hbm, wg_ref, wu_ref, wd_ref,
                      g_ref, b_ref, f_hbm, xbuf, obuf, gsem, ssem, *, rb, n_blk, alpha):
    del blk_grp_ref
    i = pl.program_id(0)
    slot = i % 2
    other = 1 - slot

    def gather_row(block, r, s):
        return pltpu.make_async_copy(h_hbm.at[pl.ds(src_ref[block * rb + r], 1), :],
                                     xbuf.at[s, pl.ds(r, 1), :], gsem.at[s])

    def scatter_row(block, r, s):
        return pltpu.make_async_copy(obuf.at[s, pl.ds(r, 1), :],
                                     f_hbm.at[pl.ds(dst_ref[(block + 2) * rb + r], 1), :], ssem.at[s])

    @pl.when(i == 0)
    def _():
        obuf[...] = jnp.zeros_like(obuf)

        def issue(r, _):
            gather_row(0, r, 0).start()
            scatter_row(-2, r, 1).start()
            return 0

        lax.fori_loop(0, rb, issue, 0)
        pltpu.make_async_copy(obuf.at[1], f_hbm.at[pl.ds(0, rb), :], ssem.at[1]).wait()

    pltpu.make_async_copy(h_hbm.at[pl.ds(0, rb), :], xbuf.at[slot], gsem.at[slot]).wait()

    @pl.when(i > 0)
    def _():
        pltpu.make_async_copy(obuf.at[slot], f_hbm.at[pl.ds(0, rb), :], ssem.at[slot]).wait()

    @pl.when(i < n_blk)
    def _():
        x = xbuf[slot, :, :D_MODEL].astype(BF16)
        w_cols = [xbuf[slot, :, D_MODEL + e:D_MODEL + e + 1] for e in range(EXPERTS_PER_GROUP)]

        def apply(e):
            gate = jnp.dot(x, wg_ref[e], preferred_element_type=F32)
            up = jnp.dot(x, wu_ref[e], preferred_element_type=F32)
            hid = (gate * _sigmoid(gate) * up).astype(BF16)
            return w_cols[e] * jnp.dot(hid, wd_ref[e], preferred_element_type=F32)

        part = rb // 2
        for k in range(EXPERTS_PER_GROUP):
            for e in range(k, min(k + 3, EXPERTS_PER_GROUP)):
                @pl.when((used_ref[i * EXPERTS_PER_GROUP + k] == e) & (n_used_ref[i] > k))
                def _():
                    if k < 2:
                        for r in range(k * part, (k + 1) * part):
                            gather_row(i + 1, r, other).start(priority=0)
                            scatter_row(i - 1, r, other).start(priority=1)
                    if k == 0:
                        obuf[slot] = apply(e)
                    else:
                        obuf[slot] += apply(e)

        obuf[slot] = _layer_norm(alpha * xbuf[slot, :, :D_MODEL] + obuf[slot], g_ref[...], b_ref[...])

    @pl.when(i == n_blk)
    def _():
        def issue(r, _):
            scatter_row(n_blk - 1, r, other).start()
            return 0

        lax.fori_loop(0, rb, issue, 0)
        pltpu.make_async_copy(obuf.at[other], f_hbm.at[pl.ds(0, rb), :], ssem.at[other]).wait()


def _group_ffn(h_ext, blk_grp, row_src, row_dst, used, n_used, w_gate, w_up, w_down, ln_g, ln_b, alpha, n_tok):
    rb = ROUTE_BLOCK
    n_blk = row_src.shape[0] // rb - 1
    wspec = lambda a, b: pl.BlockSpec((EXPERTS_PER_GROUP, a, b), lambda i, bg, rs, rd, us, nu: (bg[i], 0, 0))
    vec = pl.BlockSpec((1, D_MODEL), lambda i, bg, rs, rd, us, nu: (0, 0))
    return pl.pallas_call(
        functools.partial(_group_ffn_kernel, rb=rb, n_blk=n_blk, alpha=alpha),
        grid_spec=pltpu.PrefetchScalarGridSpec(
            num_scalar_prefetch=5,
            grid=(n_blk + 1,),
            in_specs=[pl.BlockSpec(memory_space=pl.ANY),
                      wspec(D_MODEL, D_EXPERT), wspec(D_MODEL, D_EXPERT), wspec(D_EXPERT, D_MODEL),
                      vec, vec],
            out_specs=pl.BlockSpec(memory_space=pl.ANY),
            scratch_shapes=[pltpu.VMEM((2, rb, H_EXT), F32), pltpu.VMEM((2, rb, D_MODEL), F32),
                            pltpu.SemaphoreType.DMA((2,)), pltpu.SemaphoreType.DMA((2,))]),
        out_shape=jax.ShapeDtypeStruct((n_tok + 2 * rb, D_MODEL), F32),
        compiler_params=_params("arbitrary"),
        name="group_ffn",
    )(blk_grp, row_src, row_dst, used, n_used, h_ext, w_gate, w_up, w_down, ln_g, ln_b)


def _route_rows(ri, cnt, n_tok):
    rb = ROUTE_BLOCK
    n_cls = N_GROUPS * PAIRS_PER_GROUP
    c_count = cnt[:n_cls, 0].reshape(N_GROUPS, PAIRS_PER_GROUP)
    g_count = jnp.sum(c_count, axis=1)
    g_padded = (g_count + rb - 1) // rb * rb
    g_end = jnp.cumsum(g_padded)
    c_start = ((g_end - g_padded)[:, None] + jnp.cumsum(c_count, axis=1) - c_count).reshape(n_cls)
    c_count = c_count.reshape(n_cls)
    p_rows = n_tok + N_GROUPS * rb
    order = jnp.argsort(ri[0], stable=True).astype(jnp.int32)
    filler = jnp.zeros((p_rows,), jnp.int32)
    base = jnp.concatenate([filler, order, filler])
    pos = jnp.arange(p_rows, dtype=jnp.int32)
    g_first = jnp.cumsum(g_count) - g_count
    g_start = g_end - g_padded
    row_tok = jnp.full((p_rows,), -1, jnp.int32)
    for g in range(N_GROUPS):
        window = lax.dynamic_slice(base, (p_rows + g_first[g] - g_start[g],), (p_rows,))
        row_tok = jnp.where((pos >= g_start[g]) & (pos < g_start[g] + g_count[g]), window, row_tok)
    spare = n_tok + jnp.arange(-2 * rb, p_rows, dtype=jnp.int32) % (2 * rb)
    row_src = jnp.concatenate([jnp.maximum(row_tok, 0), jnp.zeros((rb,), jnp.int32)])
    row_dst = jnp.concatenate([jnp.full((2 * rb,), -1, jnp.int32), row_tok])
    row_dst = jnp.where(row_dst < 0, spare, row_dst)
    blk_start = jnp.arange(p_rows // rb + 1, dtype=jnp.int32) * rb
    blk_grp = jnp.minimum(jnp.searchsorted(g_end, blk_start, side="right"), N_GROUPS - 1).astype(jnp.int32)
    pairs = [(a, b) for a in range(EXPERTS_PER_GROUP) for b in range(a + 1, EXPERTS_PER_GROUP)]
    member = jnp.array([[e in pairs[c % PAIRS_PER_GROUP] for e in range(EXPERTS_PER_GROUP)]
                        for c in range(n_cls)])
    overlap = ((c_start[None, :] < blk_start[:, None] + rb) & ((c_start + c_count)[None, :] > blk_start[:, None])
               & (c_count[None, :] > 0))
    need = jnp.any(overlap[:, :, None] & member[None, :, :], axis=1)
    used = jnp.argsort(~need, axis=1, stable=True).astype(jnp.int32)
    n_used = jnp.maximum(jnp.sum(need, axis=1), 2).astype(jnp.int32)
    return row_src, row_dst, blk_grp, used.reshape(-1), n_used


def _lane_row(values, offset):
    return jnp.zeros((1, LANES), F32).at[0, offset:offset + values.shape[0]].set(values.astype(F32))


def kernel(x, w_in, conv_w, a_log, dt_bias, onorm_g, w_br_a, w_br_b, w_o, ln1_g, ln1_b, w_router,
           router_bias, w_gate, w_up, w_down, ln2_g, ln2_b):
    batch, seq, d = x.shape
    depth = w_in.shape[0]
    n_tok = batch * seq
    alpha = (2 * depth) ** 0.25
    a_end = 3 * SB_WIDTH
    b_end = a_end + 4 * DN_WIDTH
    s_end = b_end + 2 * DN_HEADS
    w_a = w_in[:, :, :a_end].astype(BF16)
    w_b = w_in[:, :, a_end:b_end].astype(BF16)
    w_s = jnp.pad(w_in[:, :, b_end:s_end], ((0, 0), (0, 0), (0, LANES - 2 * DN_HEADS))).astype(BF16)
    w_g = w_in[:, :, s_end:].astype(BF16)
    w_r = jnp.pad(w_router, ((0, 0), (0, LANES - N_EXPERTS))).astype(BF16)
    r_bias = router_bias.astype(F32).reshape(N_EXPERTS, 1)
    w_br_a, w_br_b, w_o = w_br_a.astype(BF16), w_br_b.astype(BF16), w_o.astype(BF16)
    w_gate, w_up, w_down = w_gate.astype(BF16), w_up.astype(BF16), w_down.astype(BF16)

    xf = x.reshape(n_tok, d)
    for l in range(depth):
        pa, pc, pz, ps, gates = _project(xf, n_tok, w_a[l], w_b[l], w_s[l], w_g[l], conv_w[l].astype(F32), seq)
        pa = pa.reshape(batch, seq, a_end)
        pc = pc.reshape(batch, seq, 3 * DN_WIDTH)
        pz = pz.reshape(batch, seq, DN_WIDTH)
        ps = ps.reshape(batch, seq, LANES)
        o_a = _stick_breaking(pa, batch, seq).reshape(n_tok, SB_WIDTH)
        o_b = _gated_deltanet(pc, pz, ps, _lane_row(a_log[l], DN_HEADS), _lane_row(dt_bias[l], DN_HEADS),
                              onorm_g[l].astype(F32).reshape(1, DN_HEAD_DIM), batch, seq).reshape(n_tok, DN_WIDTH)
        h_ext, ri, cnt = _merge(o_a, o_b, gates, xf, w_br_a[l], w_br_b[l], w_o[l],
                                ln1_g[l].reshape(1, d), ln1_b[l].reshape(1, d), w_r, r_bias, alpha)
        row_src, row_dst, blk_grp, used, n_used = _route_rows(ri, cnt, n_tok)
        xf = _group_ffn(h_ext, blk_grp, row_src, row_dst, used, n_used, w_gate[l], w_up[l], w_down[l],
                        ln2_g[l].reshape(1, d), ln2_b[l].reshape(1, d), alpha, n_tok)
    return xf[:n_tok].reshape(batch, seq, d)
```
